```python
import math
import jax, jax.numpy as jnp
from jax import lax
import numpy as np

D_MODEL = 1024
BATCH = 2
SEQ = 8192
DEPTH = 2
DEC_BATCH = 32
DEC_SEQ = 8
PAST_LEN = 8192
PAGE_SIZE = 128

H_A = 4
HD_A = 64
W_A = H_A * 2 * HD_A
H_B = 8
HD_B = 64
W_B = H_B * HD_B
MOBA_BLOCK = 256
MOBA_TOPK = 3
N_MEM = 256
H_M = 4
HD_M = 128
W_M = H_M * HD_M
N_BUCKETS = 32
MAX_DISTANCE = 128
REL_HEADS = 2 * H_A + H_B
D_FF = 2816
N_EXPERTS = 8
TOP_K_EXPERTS = 2
D_FF_EXPERT = 2816
N_DENSE = (DEPTH + 1) // 2
N_MOE = DEPTH // 2
IN_COLS = 3 * W_A + 3 * W_B + 2 * D_MODEL
QBLK = 128
MOBA_QBLK = 64
EPS = 1e-6
NEG = -1e30

kernel_name = 'hybrid_diffattn_moba_decoder_step'


def rms_norm(x, g):
    xf = x.astype(jnp.float32)
    y = xf * lax.rsqrt(jnp.mean(xf * xf, axis=-1, keepdims=True) + EPS)
    return (y * g.astype(jnp.float32)).astype(x.dtype)


def t5_bucket(dist):
    n = jnp.maximum(dist, 0)
    max_exact = N_BUCKETS // 2
    nf = jnp.maximum(n, 1).astype(jnp.float32)
    large = max_exact + (jnp.log(nf / max_exact) / math.log(MAX_DISTANCE / max_exact)
                         * (N_BUCKETS - max_exact)).astype(jnp.int32)
    return jnp.where(n < max_exact, n, jnp.minimum(large, N_BUCKETS - 1))


def split_queries(a, blk):
    nq = a.shape[1] // blk
    return jnp.moveaxis(a.reshape(a.shape[0], nq, blk, *a.shape[2:]), 1, 0)


def join_queries(o):
    o = jnp.moveaxis(o, 0, 1)
    return o.reshape(o.shape[0], o.shape[1] * o.shape[2], *o.shape[3:])


def diff_attn_block(q, k, v, q_pos, k_pos, tab1, tab2, lam):
    f32 = jnp.float32
    scale = HD_A ** -0.5
    bucket = t5_bucket(q_pos[:, None] - k_pos[None, :])
    causal = (k_pos[None, :] <= q_pos[:, None])[None, None]

    def probs(qh, kh, tab):
        s = jnp.einsum('bqhd,bkhd->bhqk', qh.astype(f32), kh.astype(f32)) * scale
        s = s + jnp.moveaxis(tab.astype(f32)[bucket], -1, 0)[None]
        return jax.nn.softmax(jnp.where(causal, s, NEG), axis=-1)

    attn = probs(q[..., :HD_A], k[..., :HD_A], tab1) - lam * probs(q[..., HD_A:], k[..., HD_A:], tab2)
    return jnp.einsum('bhqk,bkhd->bqhd', attn, v.astype(f32))


def diff_attention(q, k, v, q_pos, tab1, tab2, lam, lam_init, subln_g):
    k_pos = jnp.arange(k.shape[1], dtype=jnp.int32)
    Tq = q.shape[1]
    if Tq > QBLK and Tq % QBLK == 0:
        o = lax.map(lambda c: diff_attn_block(c[0], k, v, c[1], k_pos, tab1, tab2, lam),
                    (split_queries(q, QBLK), q_pos.reshape(-1, QBLK)))
        o = join_queries(o)
    else:
        o = diff_attn_block(q, k, v, q_pos, k_pos, tab1, tab2, lam)
    return rms_norm(o, subln_g) * (1.0 - lam_init)


def moba_block(q, k_bh, v_bh, kmean, q_pos, tab_hb):
    f32 = jnp.float32
    B, Tq = q.shape[:2]
    nb = k_bh.shape[2]
    kk = min(MOBA_TOPK, nb)
    scale = HD_B ** -0.5
    qf = q.astype(f32)
    offs = jnp.arange(MOBA_BLOCK, dtype=jnp.int32)
    own = q_pos // MOBA_BLOCK
    gate = jnp.einsum('bqhd,bhnd->bhqn', qf, kmean)
    gate = jnp.where((jnp.arange(nb)[None, :] < own[:, None])[None, None], gate, NEG)
    _, idx = lax.top_k(gate, kk)
    sel_ok = idx < own[None, None, :, None]
    pick = jax.vmap(jax.vmap(lambda blocks, i: blocks[i]))
    k_sel = pick(k_bh, idx).astype(f32)
    v_sel = pick(v_bh, idx).astype(f32)
    pos_sel = idx[..., None] * MOBA_BLOCK + offs
    heads = jnp.arange(H_B)[None, :, None, None, None]
    s_sel = (jnp.einsum('bqhd,bhqnsd->bhqns', qf, k_sel) * scale
             + tab_hb[heads, t5_bucket(q_pos[None, None, :, None, None] - pos_sel)])
    s_sel = jnp.where(sel_ok[..., None], s_sel, NEG).reshape(B, H_B, Tq, kk * MOBA_BLOCK)
    k_own = k_bh[:, :, own].astype(f32)
    v_own = v_bh[:, :, own].astype(f32)
    pos_own = own[:, None] * MOBA_BLOCK + offs[None, :]
    s_own = (jnp.einsum('bqhd,bhqsd->bhqs', qf, k_own) * scale
             + tab_hb[:, t5_bucket(q_pos[:, None] - pos_own)][None])
    s_own = jnp.where((pos_own <= q_pos[:, None])[None, None], s_own, NEG)
    p = jax.nn.softmax(jnp.concatenate([s_sel, s_own], axis=-1), axis=-1)
    p_sel = p[..., :kk * MOBA_BLOCK].reshape(B, H_B, Tq, kk, MOBA_BLOCK)
    return (jnp.einsum('bhqns,bhqnsd->bqhd', p_sel, v_sel)
            + jnp.einsum('bhqs,bhqsd->bqhd', p[..., kk * MOBA_BLOCK:], v_own))


def moba_attention(q, k, v, q_pos, tab_hb):
    B, Tk = k.shape[:2]
    nb = -(-Tk // MOBA_BLOCK)
    pad = nb * MOBA_BLOCK - Tk

    def to_blocks(a):
        a = jnp.pad(a, ((0, 0), (0, pad), (0, 0), (0, 0)))
        return a.reshape(B, nb, MOBA_BLOCK, H_B, HD_B).transpose(0, 3, 1, 2, 4)

    k_bh, v_bh = to_blocks(k), to_blocks(v)
    kmean = jnp.mean(k_bh.astype(jnp.float32), axis=3)
    Tq = q.shape[1]
    if Tq > MOBA_QBLK and Tq % MOBA_QBLK == 0:
        o = lax.map(lambda c: moba_block(c[0], k_bh, v_bh, kmean, c[1], tab_hb),
                    (split_queries(q, MOBA_QBLK), q_pos.reshape(-1, MOBA_QBLK)))
        return join_queries(o)
    return moba_block(q, k_bh, v_bh, kmean, q_pos, tab_hb)


def project_mixers(x, norm_g, w_in):
    B, T = x.shape[:2]
    z = jnp.einsum('btd,dc->btc', rms_norm(x, norm_g), w_in)
    cuts = [W_A, 2 * W_A, 3 * W_A, 3 * W_A + W_B, 3 * W_A + 2 * W_B, 3 * W_A + 3 * W_B,
            3 * W_A + 3 * W_B + D_MODEL]
    qa, ka, va, qb, kb, vb, ga, gb = jnp.split(z, cuts, axis=-1)
    sa = (B, T, H_A, 2 * HD_A)
    sb = (B, T, H_B, HD_B)
    return (qa.reshape(sa), ka.reshape(sa), va.reshape(sa),
            qb.reshape(sb), kb.reshape(sb), vb.reshape(sb), ga, gb)


def token_mixers(x, qa, ka_all, va_all, qb, kb_all, vb_all, ga, gb, q_pos,
                 rel_bias, lam, lam_init, subln_g, w_br_a, w_br_b, w_out):
    B, T = x.shape[:2]
    f32 = jnp.float32
    o_a = diff_attention(qa, ka_all, va_all, q_pos, rel_bias[:, :H_A], rel_bias[:, H_A:2 * H_A],
                         lam, lam_init, subln_g)
    o_b = moba_attention(qb, kb_all, vb_all, q_pos, rel_bias[:, 2 * H_A:].T.astype(f32))
    ya = jnp.einsum('btc,cd->btd', o_a.reshape(B, T, W_A), w_br_a)
    yb = jnp.einsum('btc,cd->btd', o_b.reshape(B, T, W_B), w_br_b)
    mixed = jax.nn.sigmoid(ga.astype(f32)) * ya + jax.nn.sigmoid(gb.astype(f32)) * yb
    return x + jnp.einsum('btd,de->bte', mixed, w_out).astype(x.dtype)


def memory_kv(mem, norm_g, w_kv):
    B = mem.shape[0]
    k, v = jnp.split(jnp.einsum('bmd,dc->bmc', rms_norm(mem, norm_g), w_kv), 2, axis=-1)
    return k.reshape(B, -1, H_M, HD_M), v.reshape(B, -1, H_M, HD_M)


def memory_attention(x, mem_k, mem_v, norm_g, w_q, w_o):
    B, T = x.shape[:2]
    f32 = jnp.float32
    q = jnp.einsum('btd,dc->btc', rms_norm(x, norm_g), w_q).reshape(B, T, H_M, HD_M)
    s = jnp.einsum('bthd,bmhd->bhtm', q.astype(f32), mem_k.astype(f32)) * HD_M ** -0.5
    p = jax.nn.softmax(s, axis=-1)
    o = jnp.einsum('bhtm,bmhd->bthd', p, mem_v.astype(f32)).reshape(B, T, W_M)
    return x + jnp.einsum('btc,cd->btd', o, w_o).astype(x.dtype)


def swiglu(x, wg, wu, wd):
    h = jax.nn.silu(jnp.einsum('btd,df->btf', x, wg)) * jnp.einsum('btd,df->btf', x, wu)
    return jnp.einsum('btf,fd->btd', h, wd)


def dense_ffn(x, norm_g, wg, wu, wd):
    return x + swiglu(rms_norm(x, norm_g), wg, wu, wd).astype(x.dtype)


def moe_ffn(x, norm_g, router, wg, wu, wd):
    xn = rms_norm(x, norm_g)
    logits = jnp.einsum('btd,de->bte', xn, router).astype(jnp.float32)
    top_v, top_i = lax.top_k(logits, TOP_K_EXPERTS)
    top_w = jax.nn.softmax(top_v, axis=-1)
    gate = jnp.sum(jax.nn.one_hot(top_i, N_EXPERTS, dtype=jnp.float32) * top_w[..., None], axis=-2)
    y = jnp.zeros(x.shape, jnp.float32)
    for e in range(N_EXPERTS):
        y = y + gate[..., e:e + 1] * swiglu(xn, wg[e], wu[e], wd[e])
    return x + y.astype(x.dtype)


def gather_pages(cache, page_table, layer):
    rows = cache[page_table, :, layer]
    return rows.reshape(rows.shape[0], -1, *rows.shape[3:])


def setup_inputs(seed: int = 0) -> dict:
    key = jax.random.key(seed)
    ks = iter(jax.random.split(key, 48))
    f32 = jnp.float32

    def nrm(shape, scale=1.0):
        return jax.random.normal(next(ks), shape, f32) * scale

    def gain(shape):
        return 1.0 + 0.1 * jax.random.normal(next(ks), shape, f32)

    n_pages = PAST_LEN // PAGE_SIZE
    n_used = DEC_BATCH * n_pages
    n_phys = n_used + (n_used + 3) // 4
    page_table = jax.random.permutation(next(ks), n_phys)[:n_used].reshape(DEC_BATCH, n_pages).astype(jnp.int32)
    return {
        'x_prompt': nrm((BATCH, SEQ, D_MODEL)),
        'x_sample': nrm((DEC_BATCH, DEC_SEQ, D_MODEL)),
        'mem_prompt': nrm((BATCH, N_MEM, D_MODEL)),
        'cache_a_k': nrm((n_phys, PAGE_SIZE, DEPTH, H_A, 2 * HD_A)),
        'cache_a_v': nrm((n_phys, PAGE_SIZE, DEPTH, H_A, 2 * HD_A)),
        'cache_b_k': nrm((n_phys, PAGE_SIZE, DEPTH, H_B, HD_B)),
        'cache_b_v': nrm((n_phys, PAGE_SIZE, DEPTH, H_B, HD_B)),
        'cache_mem_k': nrm((DEC_BATCH, N_MEM, DEPTH, H_M, HD_M)),
        'cache_mem_v': nrm((DEC_BATCH, N_MEM, DEPTH, H_M, HD_M)),
        'page_table': page_table,
        'rel_bias': nrm((N_BUCKETS, REL_HEADS), 0.5),
        'norm_mix': gain((DEPTH, D_MODEL)),
        'w_in': nrm((DEPTH, D_MODEL, IN_COLS), D_MODEL ** -0.5),
        'lambda_q1': nrm((DEPTH, HD_A), 0.1),
        'lambda_k1': nrm((DEPTH, HD_A), 0.1),
        'lambda_q2': nrm((DEPTH, HD_A), 0.1),
        'lambda_k2': nrm((DEPTH, HD_A), 0.1),
        'subln_gain': gain((DEPTH, 2 * HD_A)),
        'w_br_a': nrm((DEPTH, W_A, D_MODEL), W_A ** -0.5),
        'w_br_b': nrm((DEPTH, W_B, D_MODEL), W_B ** -0.5),
        'w_out': nrm((DEPTH, D_MODEL, D_MODEL), D_MODEL ** -0.5),
        'norm_mem_q': gain((DEPTH, D_MODEL)),
        'norm_mem_kv': gain((DEPTH, D_MODEL)),
        'w_mem_q': nrm((DEPTH, D_MODEL, W_M), D_MODEL ** -0.5),
        'w_mem_kv': nrm((DEPTH, D_MODEL, 2 * W_M), D_MODEL ** -0.5),
        'w_mem_o': nrm((DEPTH, W_M, D_MODEL), W_M ** -0.5),
        'norm_ffn': gain((DEPTH, D_MODEL)),
        'ffn_w_gate': nrm((N_DENSE, D_MODEL, D_FF), D_MODEL ** -0.5),
        'ffn_w_up': nrm((N_DENSE, D_MODEL, D_FF), D_MODEL ** -0.5),
        'ffn_w_down': nrm((N_DENSE, D_FF, D_MODEL), D_FF ** -0.5),
        'moe_router': nrm((N_MOE, D_MODEL, N_EXPERTS), D_MODEL ** -0.5),
        'moe_w_gate': nrm((N_MOE, N_EXPERTS, D_MODEL, D_FF_EXPERT), D_MODEL ** -0.5),
        'moe_w_up': nrm((N_MOE, N_EXPERTS, D_MODEL, D_FF_EXPERT), D_MODEL ** -0.5),
        'moe_w_down': nrm((N_MOE, N_EXPERTS, D_FF_EXPERT, D_MODEL), D_FF_EXPERT ** -0.5),
        'norm_final': gain((D_MODEL,)),
    }


def reference(x_prompt, x_sample, mem_prompt, cache_a_k, cache_a_v, cache_b_k, cache_b_v,
              cache_mem_k, cache_mem_v, page_table, rel_bias, norm_mix, w_in,
              lambda_q1, lambda_k1, lambda_q2, lambda_k2, subln_gain, w_br_a, w_br_b, w_out,
              norm_mem_q, norm_mem_kv, w_mem_q, w_mem_kv, w_mem_o, norm_ffn,
              ffn_w_gate, ffn_w_up, ffn_w_down, moe_router, moe_w_gate, moe_w_up, moe_w_down,
              norm_final):
    f32 = jnp.float32
    past_len = page_table.shape[1] * PAGE_SIZE
    pos_p = jnp.arange(x_prompt.shape[1], dtype=jnp.int32)
    pos_s = past_len + jnp.arange(x_sample.shape[1], dtype=jnp.int32)
    hp, hs = x_prompt, x_sample
    ak_p, av_p, bk_p, bv_p, mk_p, mv_p = [], [], [], [], [], []
    ak_s, av_s, bk_s, bv_s = [], [], [], []
    for l in range(DEPTH):
        lam_init = 0.8 - 0.6 * math.exp(-0.3 * l)
        lam = (jnp.exp(jnp.sum(lambda_q1[l].astype(f32) * lambda_k1[l].astype(f32)))
               - jnp.exp(jnp.sum(lambda_q2[l].astype(f32) * lambda_k2[l].astype(f32))) + lam_init)

        qa, ka, va, qb, kb, vb, ga, gb = project_mixers(hp, norm_mix[l], w_in[l])
        hp = token_mixers(hp, qa, ka, va, qb, kb, vb, ga, gb, pos_p,
                          rel_bias, lam, lam_init, subln_gain[l], w_br_a[l], w_br_b[l], w_out[l])
        ak_p.append(ka)
        av_p.append(va)
        bk_p.append(kb)
        bv_p.append(vb)

        qa2, ka2, va2, qb2, kb2, vb2, ga2, gb2 = project_mixers(hs, norm_mix[l], w_in[l])
        ka_all = jnp.concatenate([gather_pages(cache_a_k, page_table, l).astype(ka2.dtype), ka2], axis=1)
        va_all = jnp.concatenate([gather_pages(cache_a_v, page_table, l).astype(va2.dtype), va2], axis=1)
        kb_all = jnp.concatenate([gather_pages(cache_b_k, page_table, l).astype(kb2.dtype), kb2], axis=1)
        vb_all = jnp.concatenate([gather_pages(cache_b_v, page_table, l).astype(vb2.dtype), vb2], axis=1)
        hs = token_mixers(hs, qa2, ka_all, va_all, qb2, kb_all, vb_all, ga2, gb2, pos_s,
                          rel_bias, lam, lam_init, subln_gain[l], w_br_a[l], w_br_b[l], w_out[l])
        ak_s.append(ka2)
        av_s.append(va2)
        bk_s.append(kb2)
        bv_s.append(vb2)

        mk, mv = memory_kv(mem_prompt, norm_mem_kv[l], w_mem_kv[l])
        mk_p.append(mk)
        mv_p.append(mv)
        hp = memory_attention(hp, mk, mv, norm_mem_q[l], w_mem_q[l], w_mem_o[l])
        hs = memory_attention(hs, cache_mem_k[:, :, l], cache_mem_v[:, :, l],
                              norm_mem_q[l], w_mem_q[l], w_mem_o[l])

        i = l // 2
        if l % 2 == 0:
            hp = dense_ffn(hp, norm_ffn[l], ffn_w_gate[i], ffn_w_up[i], ffn_w_down[i])
            hs = dense_ffn(hs, norm_ffn[l], ffn_w_gate[i], ffn_w_up[i], ffn_w_down[i])
        else:
            hp = moe_ffn(hp, norm_ffn[l], moe_router[i], moe_w_gate[i], moe_w_up[i], moe_w_down[i])
            hs = moe_ffn(hs, norm_ffn[l], moe_router[i], moe_w_gate[i], moe_w_up[i], moe_w_down[i])

    y_prompt = rms_norm(hp, norm_final)
    y_sample = rms_norm(hs, norm_final)
    new_a_k_prompt = jnp.stack(ak_p, axis=2)
    new_a_v_prompt = jnp.stack(av_p, axis=2)
    new_b_k_prompt = jnp.stack(bk_p, axis=2)
    new_b_v_prompt = jnp.stack(bv_p, axis=2)
    new_mem_k_prompt = jnp.stack(mk_p, axis=2)
    new_mem_v_prompt = jnp.stack(mv_p, axis=2)
    new_a_k_sample = jnp.stack(ak_s, axis=2)
    new_a_v_sample = jnp.stack(av_s, axis=2)
    new_b_k_sample = jnp.stack(bk_s, axis=2)
    new_b_v_sample = jnp.stack(bv_s, axis=2)
    return (y_prompt, y_sample, new_a_k_prompt, new_a_v_prompt, new_b_k_prompt, new_b_v_prompt,
            new_mem_k_prompt, new_mem_v_prompt, new_a_k_sample, new_a_v_sample,
            new_b_k_sample, new_b_v_sample)
```

```python
import functools
import math

import numpy as np
import jax
import jax.numpy as jnp
from jax import lax
from jax.experimental import pallas as pl
from jax.experimental.pallas import tpu as pltpu

F32 = jnp.float32
BF16 = jnp.bfloat16

D_MODEL = 1024
DEPTH = 2
PAGE_SIZE = 128
H_A = 4
HD_A = 64
W_A = H_A * 2 * HD_A
H_B = 8
HD_B = 64
W_B = H_B * HD_B
MOBA_BLOCK = 256
MOBA_TOPK = 3
H_M = 4
HD_M = 128
W_M = H_M * HD_M
N_BUCKETS = 32
MAX_DISTANCE = 128
N_EXPERTS = 8
TOP_K_EXPERTS = 2
IN_COLS = 3 * W_A + 3 * W_B + 2 * D_MODEL
EPS = 1e-6
NEG = -1e30

LANES = 128
SUBLANES = 8
VMEM_LIMIT_BYTES = 56 * 1024 * 1024

ATTN_TILE = MOBA_BLOCK
PAGES_PER_STEP = 8
ROW_TILE = 1024
COL_TILE = 512
FF_TILE = 256
MIX_ROW_TILE = 512

COL_QA, COL_KA, COL_VA = 0, W_A, 2 * W_A
COL_QB, COL_KB, COL_VB = 3 * W_A, 3 * W_A + W_B, 3 * W_A + 2 * W_B
COL_GA = 3 * W_A + 3 * W_B
COL_GB = COL_GA + D_MODEL


def _t5_bucket_starts():
    n = np.arange(0, 4 * MAX_DISTANCE, dtype=np.int64)
    max_exact = N_BUCKETS // 2
    nf = np.maximum(n, 1).astype(np.float32)
    large = max_exact + (np.log(nf / np.float32(max_exact))
                         / np.float32(math.log(MAX_DISTANCE / max_exact))
                         * np.float32(N_BUCKETS - max_exact)).astype(np.int32)
    bucket = np.where(n < max_exact, n, np.minimum(large, N_BUCKETS - 1))
    assert np.all(np.diff(bucket) >= 0) and bucket[-1] == N_BUCKETS - 1
    starts = [int(np.argmax(bucket >= b)) for b in range(N_BUCKETS)]
    assert starts[-1] <= MAX_DISTANCE
    return starts


BUCKET_START = _t5_bucket_starts()
FAR_DISTANCE = BUCKET_START[-1]


def _params(semantics):
    return pltpu.CompilerParams(dimension_semantics=semantics, vmem_limit_bytes=VMEM_LIMIT_BYTES)


def _rms(x, g):
    return x * lax.rsqrt(jnp.mean(x * x, axis=-1, keepdims=True) + EPS) * g


def _rel_bias(dist, rel_ref, col):
    last = rel_ref[N_BUCKETS - 1, col]
    out = jnp.zeros(dist.shape, F32)
    for b in range(N_BUCKETS - 2, -1, -1):
        out = jnp.where(dist < BUCKET_START[b + 1], rel_ref[b, col] - last, out)
    return out


def _dot_nt(a, b, precision=None):
    return lax.dot_general(a, b, (((1,), (1,)), ((), ())), precision=precision,
                           preferred_element_type=F32)


def _online_softmax_step(s, v, m_ref, l_ref, acc_ref):
    m_prev = m_ref[...]
    m_new = jnp.maximum(m_prev, jnp.max(s, axis=-1, keepdims=True))
    alpha = jnp.exp(m_prev - m_new)
    p = jnp.exp(s - m_new)
    l_ref[...] = alpha * l_ref[...] + jnp.sum(p, axis=-1, keepdims=True)
    acc_ref[...] = alpha * acc_ref[...] + jnp.dot(p.astype(BF16), v, preferred_element_type=F32)
    m_ref[...] = m_new


def _top_k_mask(gate, valid, k):
    idx = lax.broadcasted_iota(jnp.int32, gate.shape, 1).astype(F32)
    g = jnp.where(valid, gate, NEG)
    sel = jnp.zeros(gate.shape, F32)
    for _ in range(k):
        mx = jnp.max(g, axis=-1, keepdims=True)
        first = jnp.min(jnp.where(g == mx, idx, float(gate.shape[-1])), axis=-1, keepdims=True)
        pick = idx == first
        sel = jnp.where(pick, 1.0, sel)
        g = jnp.where(pick, -jnp.inf, g)
    return jnp.logical_and(sel > 0.5, valid)


def _norm_matmul_kernel(x_ref, g_ref, w_ref, o_ref, xn_ref):
    @pl.when(pl.program_id(1) == 0)
    def _():
        xn_ref[...] = _rms(x_ref[...], g_ref[...]).astype(BF16)

    o_ref[...] = jnp.dot(xn_ref[...], w_ref[...], preferred_element_type=F32)


def norm_matmul(x, g, w_bf16):
    n, d = x.shape
    c = w_bf16.shape[1]
    tm = min(ROW_TILE, n)
    tn = min(COL_TILE, c)
    assert n % tm == 0 and c % tn == 0
    return pl.pallas_call(
        _norm_matmul_kernel,
        grid=(n // tm, c // tn),
        in_specs=[pl.BlockSpec((tm, d), lambda i, j: (i, 0)),
                  pl.BlockSpec((1, d), lambda i, j: (0, 0)),
                  pl.BlockSpec((d, tn), lambda i, j: (0, j))],
        out_specs=pl.BlockSpec((tm, tn), lambda i, j: (i, j)),
        out_shape=jax.ShapeDtypeStruct((n, c), F32),
        scratch_shapes=[pltpu.VMEM((tm, d), BF16)],
        compiler_params=_params(("arbitrary", "arbitrary")),
        name="norm_matmul",
    )(x, g.reshape(1, d), w_bf16)


def _rmsnorm_kernel(x_ref, g_ref, o_ref):
    o_ref[...] = _rms(x_ref[...], g_ref[...])


def rmsnorm(x, g):
    n, d = x.shape
    tm = min(ROW_TILE, n)
    return pl.pallas_call(
        _rmsnorm_kernel,
        grid=(n // tm,),
        in_specs=[pl.BlockSpec((tm, d), lambda i: (i, 0)), pl.BlockSpec((1, d), lambda i: (0, 0))],
        out_specs=pl.BlockSpec((tm, d), lambda i: (i, 0)),
        out_shape=jax.ShapeDtypeStruct((n, d), F32),
        compiler_params=_params(("arbitrary",)),
        name="final_rmsnorm",
    )(x, g.reshape(1, d))


def _stack_streams(q, half):
    lane = lax.broadcasted_iota(jnp.int32, q.shape, 1)
    return jnp.concatenate([jnp.where(lane < half, q, 0.0), jnp.where(lane >= half, q, 0.0)], axis=0)


def _fill_tile_bias(bias_ref, rel_ref, cols, t):
    r = lax.broadcasted_iota(jnp.int32, (t, t), 0)
    c = lax.broadcasted_iota(jnp.int32, (t, t), 1)
    d_diag = r - c
    for s, col in enumerate(cols):
        bias_ref[0, s * t:(s + 1) * t, :] = _rel_bias(d_diag + t, rel_ref, col)
        bias_ref[1, s * t:(s + 1) * t, :] = jnp.where(d_diag >= 0, _rel_bias(d_diag, rel_ref, col), NEG)


def _diff_attn_kernel(rel_ref, lamv_ref, g_ref, q_ref, k_ref, v_ref, o_ref,
                      bias_ref, qs_ref, m_ref, l_ref, acc_ref, *, lam_init):
    t = ATTN_TILE
    h = pl.program_id(1)
    qi = pl.program_id(2)

    @pl.when(qi == 0)
    def _():
        _fill_tile_bias(bias_ref, rel_ref, (h, H_A + h), t)

    qs_ref[...] = _stack_streams(q_ref[0] * (HD_A ** -0.5), HD_A).astype(BF16)
    m_ref[...] = jnp.full(m_ref.shape, NEG, F32)
    l_ref[...] = jnp.zeros(l_ref.shape, F32)
    acc_ref[...] = jnp.zeros(acc_ref.shape, F32)

    def step(kb, bias):
        start = pl.multiple_of(kb * t, t)
        k = k_ref[0, pl.ds(start, t), :].astype(BF16)
        v = v_ref[0, pl.ds(start, t), :].astype(BF16)
        s = _dot_nt(qs_ref[...], k)
        if bias is not None:
            s = s + bias
        _online_softmax_step(s, v, m_ref, l_ref, acc_ref)

    def far_body(kb, carry):
        step(kb, None)
        return carry

    lax.fori_loop(0, jnp.maximum(qi - 1, 0), far_body, 0)

    @pl.when(qi >= 1)
    def _():
        step(qi - 1, bias_ref[0])

    step(qi, bias_ref[1])

    o = acc_ref[...] / l_ref[...]
    lv = lamv_ref[...]
    lam = (jnp.exp(jnp.sum(lv[0:1] * lv[1:2], axis=-1, keepdims=True))
           - jnp.exp(jnp.sum(lv[2:3] * lv[3:4], axis=-1, keepdims=True)) + lam_init)
    d = o[:t] - lam * o[t:]
    o_ref[0] = _rms(d, g_ref[...]) * (1.0 - lam_init)


def diff_attention_prompt(z, rel_bias, lamv, subln_g, lam_init):
    b, t_len, _ = z.shape
    t = ATTN_TILE
    assert t_len % t == 0
    w = 2 * HD_A
    return pl.pallas_call(
        functools.partial(_diff_attn_kernel, lam_init=lam_init),
        grid=(b, H_A, t_len // t),
        in_specs=[pl.BlockSpec(memory_space=pltpu.SMEM),
                  pl.BlockSpec((4, HD_A), lambda bi, h, qi: (0, 0)),
                  pl.BlockSpec((1, w), lambda bi, h, qi: (0, 0)),
                  pl.BlockSpec((1, t, w), lambda bi, h, qi: (bi, qi, COL_QA // w + h)),
                  pl.BlockSpec((1, t_len, w), lambda bi, h, qi: (bi, 0, COL_KA // w + h)),
                  pl.BlockSpec((1, t_len, w), lambda bi, h, qi: (bi, 0, COL_VA // w + h))],
        out_specs=pl.BlockSpec((1, t, w), lambda bi, h, qi: (bi, qi, h)),
        out_shape=jax.ShapeDtypeStruct((b, t_len, W_A), F32),
        scratch_shapes=[pltpu.VMEM((2, 2 * t, t), F32),
                        pltpu.VMEM((2 * t, w), BF16),
                        pltpu.VMEM((2 * t, 1), F32),
                        pltpu.VMEM((2 * t, 1), F32),
                        pltpu.VMEM((2 * t, w), F32)],
        compiler_params=_params(("arbitrary", "arbitrary", "arbitrary")),
        name="diff_attn_prompt",
    )(rel_bias, lamv, subln_g.reshape(1, w), z, z, z)


def _moba_attn_kernel(rel_ref, q_ref, k_ref, v_ref, o_ref,
                      bias_ref, kmean_ref, sel_ref, qs_ref, m_ref, l_ref, acc_ref, *, nb):
    t = MOBA_BLOCK
    hp = pl.program_id(1)
    qi = pl.program_id(2)

    @pl.when(qi == 0)
    def _():
        _fill_tile_bias(bias_ref, rel_ref, (2 * H_A + 2 * hp, 2 * H_A + 2 * hp + 1), t)
        kmean_ref[...] = jnp.zeros(kmean_ref.shape, F32)

        def mean_body(n, carry):
            start = pl.multiple_of(n * t, t)
            kmean_ref[pl.ds(n, 1), :] = jnp.mean(k_ref[0, pl.ds(start, t), :], axis=0, keepdims=True)
            return carry

        lax.fori_loop(0, nb, mean_body, 0)

    q = q_ref[0]
    qf = _stack_streams(q, HD_B)
    qs_ref[...] = (qf * (HD_B ** -0.5)).astype(BF16)
    gate = _dot_nt(qf, kmean_ref[...], precision=lax.Precision.HIGHEST)
    blk = lax.broadcasted_iota(jnp.int32, gate.shape, 1)
    sel_ref[...] = jnp.where(_top_k_mask(gate, blk < qi, MOBA_TOPK), 1.0, 0.0)
    m_ref[...] = jnp.full(m_ref.shape, NEG, F32)
    l_ref[...] = jnp.zeros(l_ref.shape, F32)
    acc_ref[...] = jnp.zeros(acc_ref.shape, F32)

    def step(kb, bias, gated):
        start = pl.multiple_of(kb * t, t)
        k = k_ref[0, pl.ds(start, t), :].astype(BF16)
        v = v_ref[0, pl.ds(start, t), :].astype(BF16)
        s = _dot_nt(qs_ref[...], k)
        if bias is not None:
            s = s + bias
        if gated:
            sel = sel_ref[...]
            picked = jnp.sum(jnp.where(lax.broadcasted_iota(jnp.int32, sel.shape, 1) == kb, sel, 0.0),
                             axis=-1, keepdims=True)
            s = jnp.where(picked > 0.5, s, NEG)
        _online_softmax_step(s, v, m_ref, l_ref, acc_ref)

    def far_body(kb, carry):
        step(kb, None, True)
        return carry

    lax.fori_loop(0, jnp.maximum(qi - 1, 0), far_body, 0)

    @pl.when(qi >= 1)
    def _():
        step(qi - 1, bias_ref[0], True)

    step(qi, bias_ref[1], False)

    o = acc_ref[...] / l_ref[...]
    lane = lax.broadcasted_iota(jnp.int32, (t, 2 * HD_B), 1)
    o_ref[0] = jnp.where(lane < HD_B, o[:t], o[t:])


def moba_attention_prompt(z, rel_bias):
    b, t_len, _ = z.shape
    t = MOBA_BLOCK
    assert t_len % t == 0
    nb = t_len // t
    assert nb <= LANES
    w = 2 * HD_B
    return pl.pallas_call(
        functools.partial(_moba_attn_kernel, nb=nb),
        grid=(b, H_B // 2, nb),
        in_specs=[pl.BlockSpec(memory_space=pltpu.SMEM),
                  pl.BlockSpec((1, t, w), lambda bi, hp, qi: (bi, qi, COL_QB // w + hp)),
                  pl.BlockSpec((1, t_len, w), lambda bi, hp, qi: (bi, 0, COL_KB // w + hp)),
                  pl.BlockSpec((1, t_len, w), lambda bi, hp, qi: (bi, 0, COL_VB // w + hp))],
        out_specs=pl.BlockSpec((1, t, w), lambda bi, hp, qi: (bi, qi, hp)),
        out_shape=jax.ShapeDtypeStruct((b, t_len, W_B), F32),
        scratch_shapes=[pltpu.VMEM((2, 2 * t, t), F32),
                        pltpu.VMEM((LANES, w), F32),
                        pltpu.VMEM((2 * t, LANES), F32),
                        pltpu.VMEM((2 * t, w), BF16),
                        pltpu.VMEM((2 * t, 1), F32),
                        pltpu.VMEM((2 * t, 1), F32),
                        pltpu.VMEM((2 * t, w), F32)],
        compiler_params=_params(("arbitrary", "arbitrary", "arbitrary")),
        name="moba_attn_prompt",
    )(rel_bias, z, z, z)


def _masked_query_rows(q, lane_starts, width):
    lane = lax.broadcasted_iota(jnp.int32, q.shape, 1)
    return jnp.concatenate(
        [jnp.where(jnp.logical_and(lane >= s, lane < s + width), q, 0.0) for s in lane_starts], axis=0)


def _group_bias(dist, rel_ref, cols):
    return jnp.concatenate([_rel_bias(dist, rel_ref, col) for col in cols], axis=0)


def _pad_to_page(rows):
    pad = jnp.zeros((PAGE_SIZE - rows.shape[0], rows.shape[1]), rows.dtype)
    return jnp.concatenate([rows, pad], axis=0).astype(BF16)


def _new_rows_scores(wq, k_new, rel_ref, cols):
    r_new = k_new.shape[0]
    r = lax.broadcasted_iota(jnp.int32, (r_new, PAGE_SIZE), 0)
    j = lax.broadcasted_iota(jnp.int32, (r_new, PAGE_SIZE), 1)
    s = _dot_nt(wq, _pad_to_page(k_new)) + _group_bias(r - j, rel_ref, cols)
    q_row = lax.broadcasted_iota(jnp.int32, s.shape, 0) % r_new
    return jnp.where(lax.broadcasted_iota(jnp.int32, s.shape, 1) <= q_row, s, NEG)


def _diff_decode_kernel(pt_ref, rel_ref, lamv_ref, g_ref, q_ref, kn_ref, vn_ref, *rest, lam_init):
    del pt_ref
    npg = PAGES_PER_STEP
    kp_refs, vp_refs = rest[:npg], rest[npg:2 * npg]
    o_ref, wq_ref, m_ref, l_ref, acc_ref = rest[2 * npg:]
    c = pl.program_id(1)
    last = pl.num_programs(1) - 1
    r_new = q_ref.shape[1]
    w = 2 * HD_A
    cols = [mp * H_A + h for h in range(H_A) for mp in range(2)]

    @pl.when(c == 0)
    def _():
        starts = [h * w + mp * HD_A for h in range(H_A) for mp in range(2)]
        wq_ref[...] = _masked_query_rows(q_ref[0] * (HD_A ** -0.5), starts, HD_A).astype(BF16)
        m_ref[...] = jnp.full(m_ref.shape, NEG, F32)
        l_ref[...] = jnp.zeros(l_ref.shape, F32)
        acc_ref[...] = jnp.zeros(acc_ref.shape, F32)

    for p in range(npg):
        s = _dot_nt(wq_ref[...], kp_refs[p][0].astype(BF16))
        if p == npg - 1:
            r = lax.broadcasted_iota(jnp.int32, (r_new, PAGE_SIZE), 0)
            j = lax.broadcasted_iota(jnp.int32, (r_new, PAGE_SIZE), 1)
            dist = jnp.where(c == last, r + PAGE_SIZE - j, FAR_DISTANCE)
            s = s + _group_bias(dist, rel_ref, cols)
        _online_softmax_step(s, vp_refs[p][0].astype(BF16), m_ref, l_ref, acc_ref)

    @pl.when(c == last)
    def _():
        s = _new_rows_scores(wq_ref[...], kn_ref[0], rel_ref, cols)
        _online_softmax_step(s, _pad_to_page(vn_ref[0]), m_ref, l_ref, acc_ref)

        o = acc_ref[...] / l_ref[...]
        lv = lamv_ref[...]
        lam = (jnp.exp(jnp.sum(lv[0:1] * lv[1:2], axis=-1, keepdims=True))
               - jnp.exp(jnp.sum(lv[2:3] * lv[3:4], axis=-1, keepdims=True)) + lam_init)
        for h in range(H_A):
            o1 = o[(2 * h) * r_new:(2 * h + 1) * r_new, h * w:(h + 1) * w]
            o2 = o[(2 * h + 1) * r_new:(2 * h + 2) * r_new, h * w:(h + 1) * w]
            o_ref[0, :, h * w:(h + 1) * w] = _rms(o1 - lam * o2, g_ref[...]) * (1.0 - lam_init)


def _page_specs(cache, layer, width):
    del cache
    return [pl.BlockSpec((1, PAGE_SIZE, width),
                         lambda b, c, pt, p=p: (pt[b, c * PAGES_PER_STEP + p], 0, layer))
            for p in range(PAGES_PER_STEP)]


def diff_attention_decode(z, cache_k, cache_v, page_table, layer, rel_bias, lamv, subln_g, lam_init):
    n_seq, r_new, _ = z.shape
    n_pages = page_table.shape[1]
    assert n_pages % PAGES_PER_STEP == 0 and r_new == SUBLANES
    w = 2 * HD_A
    rows = 2 * H_A * r_new
    seq_block = lambda col: pl.BlockSpec((1, r_new, W_A), lambda b, c, pt: (b, 0, col // W_A))
    grid_spec = pltpu.PrefetchScalarGridSpec(
        num_scalar_prefetch=1,
        grid=(n_seq, n_pages // PAGES_PER_STEP),
        in_specs=[pl.BlockSpec(memory_space=pltpu.SMEM),
                  pl.BlockSpec((4, HD_A), lambda b, c, pt: (0, 0)),
                  pl.BlockSpec((1, w), lambda b, c, pt: (0, 0)),
                  seq_block(COL_QA), seq_block(COL_KA), seq_block(COL_VA)]
                 + _page_specs(cache_k, layer, W_A) + _page_specs(cache_v, layer, W_A),
        out_specs=pl.BlockSpec((1, r_new, W_A), lambda b, c, pt: (b, 0, 0)),
        scratch_shapes=[pltpu.VMEM((rows, W_A), BF16),
                        pltpu.VMEM((rows, 1), F32),
                        pltpu.VMEM((rows, 1), F32),
                        pltpu.VMEM((rows, W_A), F32)])
    return pl.pallas_call(
        functools.partial(_diff_decode_kernel, lam_init=lam_init),
        grid_spec=grid_spec,
        out_shape=jax.ShapeDtypeStruct((n_seq, r_new, W_A), F32),
        compiler_params=_params(("arbitrary", "arbitrary")),
        name="diff_attn_decode",
    )(page_table, rel_bias, lamv, subln_g.reshape(1, w), z, z, z,
      *([cache_k] * PAGES_PER_STEP), *([cache_v] * PAGES_PER_STEP))


def _moba_decode_kernel(pt_ref, rel_ref, q_ref, kn_ref, vn_ref, *rest):
    del pt_ref
    npg = PAGES_PER_STEP
    kp_refs, vp_refs = rest[:npg], rest[npg:2 * npg]
    o_ref, wq_ref, wqf_ref, kmean_ref, mblk_ref, lblk_ref, accblk_ref = rest[2 * npg:]
    c = pl.program_id(1)
    last = pl.num_programs(1) - 1
    r_new = q_ref.shape[1]
    nb = accblk_ref.shape[0]
    pages_per_block = MOBA_BLOCK // PAGE_SIZE
    blocks_per_step = npg // pages_per_block
    cols = [2 * H_A + h for h in range(H_B)]

    @pl.when(c == 0)
    def _():
        qf = _masked_query_rows(q_ref[0], [h * HD_B for h in range(H_B)], HD_B)
        wqf_ref[...] = qf
        wq_ref[...] = (qf * (HD_B ** -0.5)).astype(BF16)
        kmean_ref[...] = jnp.zeros(kmean_ref.shape, F32)
        mblk_ref[...] = jnp.full(mblk_ref.shape, NEG, F32)
        lblk_ref[...] = jnp.zeros(lblk_ref.shape, F32)

    blk_iota = lax.broadcasted_iota(jnp.int32, mblk_ref.shape, 1)
    for jb in range(blocks_per_step):
        n = c * blocks_per_step + jb
        pages = range(jb * pages_per_block, (jb + 1) * pages_per_block)
        kf = [kp_refs[p][0] for p in pages]
        s_parts = [_dot_nt(wq_ref[...], k.astype(BF16)) for k in kf]
        if jb == blocks_per_step - 1:
            r = lax.broadcasted_iota(jnp.int32, (r_new, PAGE_SIZE), 0)
            j = lax.broadcasted_iota(jnp.int32, (r_new, PAGE_SIZE), 1)
            dist = jnp.where(c == last, r + PAGE_SIZE - j, FAR_DISTANCE)
            s_parts[-1] = s_parts[-1] + _group_bias(dist, rel_ref, cols)
        s = jnp.concatenate(s_parts, axis=-1)
        m = jnp.max(s, axis=-1, keepdims=True)
        pexp = jnp.exp(s - m)
        acc = sum(jnp.dot(pexp[:, i * PAGE_SIZE:(i + 1) * PAGE_SIZE].astype(BF16),
                          vp_refs[p][0].astype(BF16), preferred_element_type=F32)
                  for i, p in enumerate(pages))
        accblk_ref[n] = acc
        mblk_ref[...] = jnp.where(blk_iota == n, m, mblk_ref[...])
        lblk_ref[...] = jnp.where(blk_iota == n, jnp.sum(pexp, axis=-1, keepdims=True), lblk_ref[...])
        kmean_ref[pl.ds(n, 1), :] = sum(jnp.sum(k, axis=0, keepdims=True) for k in kf) * (1.0 / MOBA_BLOCK)

    @pl.when(c == last)
    def _():
        gate = _dot_nt(wqf_ref[...], kmean_ref[...], precision=lax.Precision.HIGHEST)
        sel = _top_k_mask(gate, blk_iota < nb, MOBA_TOPK)
        s_own = _new_rows_scores(wq_ref[...], kn_ref[0], rel_ref, cols)
        mblk = jnp.where(sel, mblk_ref[...], NEG)
        m_all = jnp.maximum(jnp.max(mblk, axis=-1, keepdims=True), jnp.max(s_own, axis=-1, keepdims=True))
        wgt = jnp.where(sel, jnp.exp(mblk - m_all), 0.0)
        p_own = jnp.exp(s_own - m_all)
        denom = (jnp.sum(wgt * lblk_ref[...], axis=-1, keepdims=True)
                 + jnp.sum(p_own, axis=-1, keepdims=True))
        acc = jnp.dot(p_own.astype(BF16), _pad_to_page(vn_ref[0]), preferred_element_type=F32)
        for n in range(nb):
            acc = acc + wgt[:, n:n + 1] * accblk_ref[n]
        o = acc / denom
        lane = lax.broadcasted_iota(jnp.int32, (r_new, W_B), 1)
        out = jnp.zeros((r_new, W_B), F32)
        for h in range(H_B):
            in_head = jnp.logical_and(lane >= h * HD_B, lane < (h + 1) * HD_B)
            out = out + jnp.where(in_head, o[h * r_new:(h + 1) * r_new, :], 0.0)
        o_ref[0] = out


def moba_attention_decode(z, cache_k, cache_v, page_table, layer, rel_bias):
    n_seq, r_new, _ = z.shape
    n_pages = page_table.shape[1]
    past = n_pages * PAGE_SIZE
    assert n_pages % PAGES_PER_STEP == 0 and past % MOBA_BLOCK == 0 and r_new == SUBLANES
    assert PAGES_PER_STEP % (MOBA_BLOCK // PAGE_SIZE) == 0 and r_new <= MOBA_BLOCK
    nb = past // MOBA_BLOCK
    assert nb <= LANES
    rows = H_B * r_new
    seq_block = lambda col: pl.BlockSpec((1, r_new, W_B), lambda b, c, pt: (b, 0, col // W_B))
    grid_spec = pltpu.PrefetchScalarGridSpec(
        num_scalar_prefetch=1,
        grid=(n_seq, n_pages // PAGES_PER_STEP),
        in_specs=[pl.BlockSpec(memory_space=pltpu.SMEM),
                  seq_block(COL_QB), seq_block(COL_KB), seq_block(COL_VB)]
                 + _page_specs(cache_k, layer, W_B) + _page_specs(cache_v, layer, W_B),
        out_specs=pl.BlockSpec((1, r_new, W_B), lambda b, c, pt: (b, 0, 0)),
        scratch_shapes=[pltpu.VMEM((rows, W_B), BF16),
                        pltpu.VMEM((rows, W_B), F32),
                        pltpu.VMEM((LANES, W_B), F32),
                        pltpu.VMEM((rows, LANES), F32),
                        pltpu.VMEM((rows, LANES), F32),
                        pltpu.VMEM((nb, rows, W_B), F32)])
    return pl.pallas_call(
        _moba_decode_kernel,
        grid_spec=grid_spec,
        out_shape=jax.ShapeDtypeStruct((n_seq, r_new, W_B), F32),
        compiler_params=_params(("arbitrary", "arbitrary")),
        name="moba_attn_decode",
    )(page_table, rel_bias, z, z, z, *([cache_k] * PAGES_PER_STEP), *([cache_v] * PAGES_PER_STEP))


def _sigmoid(x):
    return 1.0 / (1.0 + jnp.exp(-x))


def _mix_out_kernel(oa_ref, ob_ref, ga_ref, gb_ref, x_ref, wa_ref, wb_ref, wo_ref, o_ref):
    ya = jnp.dot(oa_ref[...].astype(BF16), wa_ref[...], preferred_element_type=F32)
    yb = jnp.dot(ob_ref[...].astype(BF16), wb_ref[...], preferred_element_type=F32)
    mixed = _sigmoid(ga_ref[...]) * ya + _sigmoid(gb_ref[...]) * yb
    o_ref[...] = x_ref[...] + jnp.dot(mixed.astype(BF16), wo_ref[...], preferred_element_type=F32)


def mix_out(oa, ob, z, x, wa, wb, wo):
    n, d = x.shape
    tm = min(MIX_ROW_TILE, n)
    assert n % tm == 0
    full = lambda a: pl.BlockSpec(a.shape, lambda i: (0, 0))
    return pl.pallas_call(
        _mix_out_kernel,
        grid=(n // tm,),
        in_specs=[pl.BlockSpec((tm, W_A), lambda i: (i, 0)),
                  pl.BlockSpec((tm, W_B), lambda i: (i, 0)),
                  pl.BlockSpec((tm, d), lambda i: (i, COL_GA // d)),
                  pl.BlockSpec((tm, d), lambda i: (i, COL_GB // d)),
                  pl.BlockSpec((tm, d), lambda i: (i, 0)),
                  full(wa), full(wb), full(wo)],
        out_specs=pl.BlockSpec((tm, d), lambda i: (i, 0)),
        out_shape=jax.ShapeDtypeStruct((n, d), F32),
        compiler_params=_params(("arbitrary",)),
        name="mix_out",
    )(oa, ob, z, z, x, wa, wb, wo)


def _mem_attn_kernel(x_ref, g_ref, k_ref, v_ref, wq_ref, wo_ref, o_ref):
    x = x_ref[0]
    q = jnp.dot(_rms(x, g_ref[...]).astype(BF16), wq_ref[...], preferred_element_type=F32)
    heads = []
    for h in range(H_M):
        sl = slice(h * HD_M, (h + 1) * HD_M)
        s = _dot_nt((q[:, sl] * (HD_M ** -0.5)).astype(BF16), k_ref[0, :, sl].astype(BF16))
        p = jnp.exp(s - jnp.max(s, axis=-1, keepdims=True))
        oh = jnp.dot(p.astype(BF16), v_ref[0, :, sl].astype(BF16), preferred_element_type=F32)
        heads.append(oh / jnp.sum(p, axis=-1, keepdims=True))
    o = jnp.concatenate(heads, axis=-1)
    o_ref[0] = x + jnp.dot(o.astype(BF16), wo_ref[...], preferred_element_type=F32)


def memory_attention(x, g, mem_k, k_col, mem_v, v_col, wq, wo):
    grp, t_len, d = x.shape
    n_mem = mem_k.shape[1]
    tm = min(MIX_ROW_TILE, t_len)
    assert t_len % tm == 0
    return pl.pallas_call(
        _mem_attn_kernel,
        grid=(grp, t_len // tm),
        in_specs=[pl.BlockSpec((1, tm, d), lambda b, i: (b, i, 0)),
                  pl.BlockSpec((1, d), lambda b, i: (0, 0)),
                  pl.BlockSpec((1, n_mem, W_M), lambda b, i: (b, 0, k_col)),
                  pl.BlockSpec((1, n_mem, W_M), lambda b, i: (b, 0, v_col)),
                  pl.BlockSpec(wq.shape, lambda b, i: (0, 0)),
                  pl.BlockSpec(wo.shape, lambda b, i: (0, 0))],
        out_specs=pl.BlockSpec((1, tm, d), lambda b, i: (b, i, 0)),
        out_shape=jax.ShapeDtypeStruct((grp, t_len, d), F32),
        compiler_params=_params(("arbitrary", "arbitrary")),
        name="memory_attention",
    )(x, g.reshape(1, d), mem_k, mem_v, wq, wo)


def _swiglu_hidden(xn, wg, wu):
    a = jnp.dot(xn, wg, preferred_element_type=F32)
    return a * _sigmoid(a) * jnp.dot(xn, wu, preferred_element_type=F32)


def _dense_ffn_kernel(x_ref, g_ref, wg_ref, wu_ref, wd_ref, o_ref, xn_ref, acc_ref):
    j = pl.program_id(1)

    @pl.when(j == 0)
    def _():
        xn_ref[...] = _rms(x_ref[...], g_ref[...]).astype(BF16)
        acc_ref[...] = jnp.zeros(acc_ref.shape, F32)

    hid = _swiglu_hidden(xn_ref[...], wg_ref[...], wu_ref[...])
    acc_ref[...] += jnp.dot(hid.astype(BF16), wd_ref[...], preferred_element_type=F32)

    @pl.when(j == pl.num_programs(1) - 1)
    def _():
        o_ref[...] = x_ref[...] + acc_ref[...]


def dense_ffn(x, g, wg, wu, wd):
    n, d = x.shape
    f = wg.shape[1]
    tm = min(ROW_TILE, n)
    tf = FF_TILE
    assert n % tm == 0 and f % tf == 0
    return pl.pallas_call(
        _dense_ffn_kernel,
        grid=(n // tm, f // tf),
        in_specs=[pl.BlockSpec((tm, d), lambda i, j: (i, 0)),
                  pl.BlockSpec((1, d), lambda i, j: (0, 0)),
                  pl.BlockSpec((d, tf), lambda i, j: (0, j)),
                  pl.BlockSpec((d, tf), lambda i, j: (0, j)),
                  pl.BlockSpec((tf, d), lambda i, j: (j, 0))],
        out_specs=pl.BlockSpec((tm, d), lambda i, j: (i, 0)),
        out_shape=jax.ShapeDtypeStruct((n, d), F32),
        scratch_shapes=[pltpu.VMEM((tm, d), BF16), pltpu.VMEM((tm, d), F32)],
        compiler_params=_params(("arbitrary", "arbitrary")),
        name="dense_ffn",
    )(x, g.reshape(1, d), wg, wu, wd)


def _moe_ffn_kernel(x_ref, g_ref, r_ref, wg_ref, wu_ref, wd_ref, o_ref, xn_ref, gate_ref, acc_ref):
    e = pl.program_id(1)
    j = pl.program_id(2)

    @pl.when(jnp.logical_and(e == 0, j == 0))
    def _():
        xn = _rms(x_ref[...], g_ref[...])
        xn_ref[...] = xn.astype(BF16)
        acc_ref[...] = jnp.zeros(acc_ref.shape, F32)
        logits = jnp.dot(xn, r_ref[...], precision=lax.Precision.HIGHEST, preferred_element_type=F32)
        lane = lax.broadcasted_iota(jnp.int32, logits.shape, 1).astype(F32)
        logits = jnp.where(lane < N_EXPERTS, logits, -jnp.inf)
        v1 = jnp.max(logits, axis=-1, keepdims=True)
        i1 = jnp.min(jnp.where(logits == v1, lane, float(LANES)), axis=-1, keepdims=True)
        rest = jnp.where(lane == i1, -jnp.inf, logits)
        v2 = jnp.max(rest, axis=-1, keepdims=True)
        i2 = jnp.min(jnp.where(rest == v2, lane, float(LANES)), axis=-1, keepdims=True)
        e2 = jnp.exp(v2 - v1)
        denom = 1.0 + e2
        gate_ref[...] = jnp.where(lane == i1, 1.0 / denom, 0.0) + jnp.where(lane == i2, e2 / denom, 0.0)

    gates = gate_ref[...]
    lane = lax.broadcasted_iota(jnp.int32, gates.shape, 1)
    ge = jnp.sum(jnp.where(lane == e, gates, 0.0), axis=-1, keepdims=True)
    hid = _swiglu_hidden(xn_ref[...], wg_ref[0], wu_ref[0]) * ge
    acc_ref[...] += jnp.dot(hid.astype(BF16), wd_ref[0], preferred_element_type=F32)

    @pl.when(jnp.logical_and(e == pl.num_programs(1) - 1, j == pl.num_programs(2) - 1))
    def _():
        o_ref[...] = x_ref[...] + acc_ref[...]


def moe_ffn(x, g, router_padded, wg, wu, wd):
    n, d = x.shape
    n_exp, _, f = wg.shape
    tm = min(ROW_TILE, n)
    tf = FF_TILE
    assert n % tm == 0 and f % tf == 0 and n_exp == N_EXPERTS
    return pl.pallas_call(
        _moe_ffn_kernel,
        grid=(n // tm, n_exp, f // tf),
        in_specs=[pl.BlockSpec((tm, d), lambda i, e, j: (i, 0)),
                  pl.BlockSpec((1, d), lambda i, e, j: (0, 0)),
                  pl.BlockSpec((d, LANES), lambda i, e, j: (0, 0)),
                  pl.BlockSpec((1, d, tf), lambda i, e, j: (e, 0, j)),
                  pl.BlockSpec((1, d, tf), lambda i, e, j: (e, 0, j)),
                  pl.BlockSpec((1, tf, d), lambda i, e, j: (e, j, 0))],
        out_specs=pl.BlockSpec((tm, d), lambda i, e, j: (i, 0)),
        out_shape=jax.ShapeDtypeStruct((n, d), F32),
        scratch_shapes=[pltpu.VMEM((tm, d), BF16), pltpu.VMEM((tm, LANES), F32), pltpu.VMEM((tm, d), F32)],
        compiler_params=_params(("arbitrary", "arbitrary", "arbitrary")),
        name="moe_ffn",
    )(x, g.reshape(1, d), router_padded, wg, wu, wd)


def kernel(x_prompt, x_sample, mem_prompt, cache_a_k, cache_a_v, cache_b_k, cache_b_v, cache_mem_k, cache_mem_v, page_table, rel_bias, norm_mix, w_in, lambda_q1, lambda_k1, lambda_q2, lambda_k2, subln_gain, w_br_a, w_br_b, w_out, norm_mem_q, norm_mem_kv, w_mem_q, w_mem_kv, w_mem_o, norm_ffn, ffn_w_gate, ffn_w_up, ffn_w_down, moe_router, moe_w_gate, moe_w_up, moe_w_down, norm_final):
    bsz, seq, d = x_prompt.shape
    n_seq, r_new, _ = x_sample.shape
    n_mem = mem_prompt.shape[1]
    n_phys = cache_a_k.shape[0]
    depth = w_in.shape[0]

    ca_k = cache_a_k.reshape(n_phys, PAGE_SIZE, depth * W_A)
    ca_v = cache_a_v.reshape(n_phys, PAGE_SIZE, depth * W_A)
    cb_k = cache_b_k.reshape(n_phys, PAGE_SIZE, depth * W_B)
    cb_v = cache_b_v.reshape(n_phys, PAGE_SIZE, depth * W_B)
    cm_k = cache_mem_k.reshape(n_seq, n_mem, depth * W_M)
    cm_v = cache_mem_v.reshape(n_seq, n_mem, depth * W_M)

    hp = x_prompt.reshape(bsz * seq, d)
    hs = x_sample.reshape(n_seq * r_new, d)
    zs_p, zs_s, mkvs = [], [], []
    for l in range(depth):
        lam_init = 0.8 - 0.6 * math.exp(-0.3 * l)
        lamv = jnp.stack([lambda_q1[l], lambda_k1[l], lambda_q2[l], lambda_k2[l]]).astype(F32)
        w_in_l = w_in[l].astype(BF16)
        wa, wb, wo = w_br_a[l].astype(BF16), w_br_b[l].astype(BF16), w_out[l].astype(BF16)

        zp = norm_matmul(hp, norm_mix[l], w_in_l)
        zp3 = zp.reshape(bsz, seq, IN_COLS)
        oa = diff_attention_prompt(zp3, rel_bias, lamv, subln_gain[l], lam_init)
        ob = moba_attention_prompt(zp3, rel_bias)
        hp = mix_out(oa.reshape(bsz * seq, W_A), ob.reshape(bsz * seq, W_B), zp, hp, wa, wb, wo)
        zs_p.append(zp3)

        zs = norm_matmul(hs, norm_mix[l], w_in_l)
        zs3 = zs.reshape(n_seq, r_new, IN_COLS)
        oa_s = diff_attention_decode(zs3, ca_k, ca_v, page_table, l, rel_bias, lamv, subln_gain[l], lam_init)
        ob_s = moba_attention_decode(zs3, cb_k, cb_v, page_table, l, rel_bias)
        hs = mix_out(oa_s.reshape(n_seq * r_new, W_A), ob_s.reshape(n_seq * r_new, W_B), zs, hs, wa, wb, wo)
        zs_s.append(zs3)

        mkv = norm_matmul(mem_prompt.reshape(bsz * n_mem, d), norm_mem_kv[l], w_mem_kv[l].astype(BF16))
        mkv3 = mkv.reshape(bsz, n_mem, 2 * W_M)
        mkvs.append(mkv3)
        wq, wmo = w_mem_q[l].astype(BF16), w_mem_o[l].astype(BF16)
        hp = memory_attention(hp.reshape(bsz, seq, d), norm_mem_q[l], mkv3, 0, mkv3, 1, wq, wmo).reshape(bsz * seq, d)
        hs = memory_attention(hs.reshape(n_seq, r_new, d), norm_mem_q[l], cm_k, l, cm_v, l, wq, wmo).reshape(n_seq * r_new, d)

        i = l // 2
        if l % 2 == 0:
            wg, wu, wd = ffn_w_gate[i].astype(BF16), ffn_w_up[i].astype(BF16), ffn_w_down[i].astype(BF16)
            hp = dense_ffn(hp, norm_ffn[l], wg, wu, wd)
            hs = dense_ffn(hs, norm_ffn[l], wg, wu, wd)
        else:
            wg, wu, wd = moe_w_gate[i].astype(BF16), moe_w_up[i].astype(BF16), moe_w_down[i].astype(BF16)
            router = jnp.pad(moe_router[i].astype(F32), ((0, 0), (0, LANES - N_EXPERTS)))
            hp = moe_ffn(hp, norm_ffn[l], router, wg, wu, wd)
            hs = moe_ffn(hs, norm_ffn[l], router, wg, wu, wd)

    y_prompt = rmsnorm(hp, norm_final).reshape(bsz, seq, d)
    y_sample = rmsnorm(hs, norm_final).reshape(n_seq, r_new, d)

    def stacked(zs, col, width, heads):
        rows = jnp.stack([z[:, :, col:col + width] for z in zs], axis=2)
        return rows.reshape(rows.shape[0], rows.shape[1], depth, heads, width // heads)

    new_mem_k = jnp.stack([m[:, :, :W_M] for m in mkvs], axis=2).reshape(bsz, n_mem, depth, H_M, HD_M)
    new_mem_v = jnp.stack([m[:, :, W_M:] for m in mkvs], axis=2).reshape(bsz, n_mem, depth, H_M, HD_M)
    return (y_prompt, y_sample,
            stacked(zs_p, COL_KA, W_A, H_A), stacked(zs_p, COL_VA, W_A, H_A),
            stacked(zs_p, COL_KB, W_B, H_B), stacked(zs_p, COL_VB, W_B, H_B),
            new_mem_k, new_mem_v,
            stacked(zs_s, COL_KA, W_A, H_A), stacked(zs_s, COL_VA, W_A, H_A),
            stacked(zs_s, COL_KB, W_B, H_B), stacked(zs_s, COL_VB, W_B, H_B))
```

```python
import functools
import math

import numpy as np
import jax
import jax.numpy as jnp
from jax import lax
from jax.experimental import pallas as pl
from jax.experimental.pallas import tpu as pltpu

F32 = jnp.float32
BF16 = jnp.bfloat16

D_MODEL = 1024
DEPTH = 2
PAGE_SIZE = 128
H_A = 4
HD_A = 64
W_A = H_A * 2 * HD_A
H_B = 8
HD_B = 64
W_B = H_B * HD_B
MOBA_BLOCK = 256
MOBA_TOPK = 3
H_M = 4
HD_M = 128
W_M = H_M * HD_M
N_BUCKETS = 32
MAX_DISTANCE = 128
N_EXPERTS = 8
TOP_K_EXPERTS = 2
IN_COLS = 3 * W_A + 3 * W_B + 2 * D_MODEL
EPS = 1e-6
NEG = -1e30
LOG2E = math.log2(math.e)

LANES = 128
SUBLANES = 8
VMEM_LIMIT_BYTES = 56 * 1024 * 1024

ATTN_TILE = 2 * MOBA_BLOCK
ATTN_ROW_CHUNK = MOBA_BLOCK
PAGES_PER_STEP = 8
ROW_TILE = 1024
COL_TILE = 512
FF_TILE = 256
MIX_ROW_TILE = 512

COL_QA, COL_KA, COL_VA = 0, W_A, 2 * W_A
COL_QB, COL_KB, COL_VB = 3 * W_A, 3 * W_A + W_B, 3 * W_A + 2 * W_B
COL_GA = 3 * W_A + 3 * W_B
COL_GB = COL_GA + D_MODEL


def _t5_bucket_starts():
    n = np.arange(0, 4 * MAX_DISTANCE, dtype=np.int64)
    max_exact = N_BUCKETS // 2
    nf = np.maximum(n, 1).astype(np.float32)
    large = max_exact + (np.log(nf / np.float32(max_exact))
                         / np.float32(math.log(MAX_DISTANCE / max_exact))
                         * np.float32(N_BUCKETS - max_exact)).astype(np.int32)
    bucket = np.where(n < max_exact, n, np.minimum(large, N_BUCKETS - 1))
    assert np.all(np.diff(bucket) >= 0) and bucket[-1] == N_BUCKETS - 1
    starts = [int(np.argmax(bucket >= b)) for b in range(N_BUCKETS)]
    assert starts[-1] <= MAX_DISTANCE
    return starts


BUCKET_START = _t5_bucket_starts()
FAR_DISTANCE = BUCKET_START[-1]


def _params(semantics):
    return pltpu.CompilerParams(dimension_semantics=semantics, vmem_limit_bytes=VMEM_LIMIT_BYTES)


def _rms(x, g):
    return x * lax.rsqrt(jnp.mean(x * x, axis=-1, keepdims=True) + EPS) * g


def _rel_bias(dist, rel_ref, col, scale=1.0):
    last = rel_ref[N_BUCKETS - 1, col]
    out = jnp.zeros(dist.shape, F32)
    for b in range(N_BUCKETS - 2, -1, -1):
        out = jnp.where(dist < BUCKET_START[b + 1], (rel_ref[b, col] - last) * scale, out)
    return out


def _dot_nt(a, b, precision=None):
    return lax.dot_general(a, b, (((1,), (1,)), ((), ())), precision=precision,
                           preferred_element_type=F32)


def _online_softmax_step(s, v, m_ref, l_ref, acc_ref):
    m_prev = m_ref[...]
    m_new = jnp.maximum(m_prev, jnp.max(s, axis=-1, keepdims=True))
    alpha = jnp.exp(m_prev - m_new)
    p = jnp.exp(s - m_new)
    l_ref[...] = alpha * l_ref[...] + jnp.sum(p, axis=-1, keepdims=True)
    acc_ref[...] = alpha * acc_ref[...] + jnp.dot(p.astype(BF16), v, preferred_element_type=F32)
    m_ref[...] = m_new


def _top_k_mask(gate, valid, k):
    idx = lax.broadcasted_iota(jnp.int32, gate.shape, 1).astype(F32)
    g = jnp.where(valid, gate, NEG)
    sel = jnp.zeros(gate.shape, F32)
    for _ in range(k):
        mx = jnp.max(g, axis=-1, keepdims=True)
        first = jnp.min(jnp.where(g == mx, idx, float(gate.shape[-1])), axis=-1, keepdims=True)
        pick = idx == first
        sel = jnp.where(pick, 1.0, sel)
        g = jnp.where(pick, -jnp.inf, g)
    return jnp.logical_and(sel > 0.5, valid)


def _norm_matmul_kernel(x_ref, g_ref, w_ref, o_ref, xn_ref):
    @pl.when(pl.program_id(1) == 0)
    def _():
        xn_ref[...] = _rms(x_ref[...], g_ref[...]).astype(BF16)

    o_ref[...] = jnp.dot(xn_ref[...], w_ref[...], preferred_element_type=F32)


def norm_matmul(x, g, w_bf16):
    n, d = x.shape
    c = w_bf16.shape[1]
    tm = min(ROW_TILE, n)
    tn = min(COL_TILE, c)
    assert n % tm == 0 and c % tn == 0
    return pl.pallas_call(
        _norm_matmul_kernel,
        grid=(n // tm, c // tn),
        in_specs=[pl.BlockSpec((tm, d), lambda i, j: (i, 0)),
                  pl.BlockSpec((1, d), lambda i, j: (0, 0)),
                  pl.BlockSpec((d, tn), lambda i, j: (0, j))],
        out_specs=pl.BlockSpec((tm, tn), lambda i, j: (i, j)),
        out_shape=jax.ShapeDtypeStruct((n, c), F32),
        scratch_shapes=[pltpu.VMEM((tm, d), BF16)],
        compiler_params=_params(("arbitrary", "arbitrary")),
        name="norm_matmul",
    )(x, g.reshape(1, d), w_bf16)


def _rmsnorm_kernel(x_ref, g_ref, o_ref):
    o_ref[...] = _rms(x_ref[...], g_ref[...])


def rmsnorm(x, g):
    n, d = x.shape
    tm = min(ROW_TILE, n)
    return pl.pallas_call(
        _rmsnorm_kernel,
        grid=(n // tm,),
        in_specs=[pl.BlockSpec((tm, d), lambda i: (i, 0)), pl.BlockSpec((1, d), lambda i: (0, 0))],
        out_specs=pl.BlockSpec((tm, d), lambda i: (i, 0)),
        out_shape=jax.ShapeDtypeStruct((n, d), F32),
        compiler_params=_params(("arbitrary",)),
        name="final_rmsnorm",
    )(x, g.reshape(1, d))


def _stack_streams(q, half):
    lane = lax.broadcasted_iota(jnp.int32, q.shape, 1)
    return jnp.concatenate([jnp.where(lane < half, q, 0.0), jnp.where(lane >= half, q, 0.0)], axis=0)


def _fill_tile_bias(bias_ref, rel_ref, cols, t):
    r = lax.broadcasted_iota(jnp.int32, (t, t), 0)
    c = lax.broadcasted_iota(jnp.int32, (t, t), 1)
    d_diag = r - c
    for s, col in enumerate(cols):
        bias_ref[0, s * t:(s + 1) * t, :] = _rel_bias(d_diag + t, rel_ref, col, LOG2E)
        bias_ref[1, s * t:(s + 1) * t, :] = jnp.where(d_diag >= 0, _rel_bias(d_diag, rel_ref, col, LOG2E), NEG)


def _prepare_kv(k_ref, v_ref, kb_ref, va_ref, t):
    n_blocks = k_ref.shape[1] // t
    w = v_ref.shape[2]

    def body(n, carry):
        rows = pl.ds(pl.multiple_of(n * t, t), t)
        kb_ref[rows, :] = k_ref[0, rows, :].astype(BF16)
        va_ref[rows, :w] = v_ref[0, rows, :].astype(BF16)
        va_ref[rows, w:] = jnp.ones((t, w), BF16)
        return carry

    lax.fori_loop(0, n_blocks, body, 0)


def _flash_step(s, va, m_ref, acc_ref):
    m_prev = m_ref[...]
    m_new = jnp.maximum(m_prev, jnp.max(s, axis=-1, keepdims=True))
    alpha = jnp.exp2(m_prev - m_new)
    p = jnp.exp2(s - pltpu.repeat(m_new, s.shape[1] // LANES, axis=1))
    acc_ref[...] = (pltpu.repeat(alpha, acc_ref.shape[1] // LANES, axis=1) * acc_ref[...]
                    + jnp.dot(p.astype(BF16), va, preferred_element_type=F32))
    m_ref[...] = m_new


def _diff_attn_kernel(rel_ref, lamv_ref, g_ref, q_ref, k_ref, v_ref, o_ref,
                      bias_ref, kb_ref, va_ref, qs_ref, m_ref, acc_ref, *, lam_init):
    t = ATTN_TILE
    w = 2 * HD_A
    h = pl.program_id(1)
    qi = pl.program_id(2)

    @pl.when(qi == 0)
    def _():
        _fill_tile_bias(bias_ref, rel_ref, (h, H_A + h), t)
        _prepare_kv(k_ref, v_ref, kb_ref, va_ref, t)

    qs_ref[...] = _stack_streams(q_ref[0] * (HD_A ** -0.5 * LOG2E), HD_A).astype(BF16)
    m_ref[...] = jnp.full(m_ref.shape, NEG, F32)
    acc_ref[...] = jnp.zeros(acc_ref.shape, F32)

    def step(kb, bias):
        keys = pl.ds(pl.multiple_of(kb * t, t), t)
        for c in range(2 * t // ATTN_ROW_CHUNK):
            rows = slice(c * ATTN_ROW_CHUNK, (c + 1) * ATTN_ROW_CHUNK)
            s = _dot_nt(qs_ref[rows, :], kb_ref[keys, :])
            if bias is not None:
                s = s + bias[rows, :]
            _flash_step(s, va_ref[keys, :], m_ref.at[rows, :], acc_ref.at[rows, :])

    def far_body(kb, carry):
        step(kb, None)
        return carry

    lax.fori_loop(0, jnp.maximum(qi - 1, 0), far_body, 0)

    @pl.when(qi >= 1)
    def _():
        step(qi - 1, bias_ref[0])

    step(qi, bias_ref[1])

    acc = acc_ref[...]
    o = acc[:, :w] / acc[:, w:]
    lv = lamv_ref[...]
    lam = (jnp.exp(jnp.sum(lv[0:1] * lv[1:2], axis=-1, keepdims=True))
           - jnp.exp(jnp.sum(lv[2:3] * lv[3:4], axis=-1, keepdims=True)) + lam_init)
    d = o[:t] - lam * o[t:]
    o_ref[0] = _rms(d, g_ref[...]) * (1.0 - lam_init)


def diff_attention_prompt(z, rel_bias, lamv, subln_g, lam_init):
    b, t_len, _ = z.shape
    t = ATTN_TILE
    assert t_len % t == 0
    w = 2 * HD_A
    return pl.pallas_call(
        functools.partial(_diff_attn_kernel, lam_init=lam_init),
        grid=(b, H_A, t_len // t),
        in_specs=[pl.BlockSpec(memory_space=pltpu.SMEM),
                  pl.BlockSpec((4, HD_A), lambda bi, h, qi: (0, 0)),
                  pl.BlockSpec((1, w), lambda bi, h, qi: (0, 0)),
                  pl.BlockSpec((1, t, w), lambda bi, h, qi: (bi, qi, COL_QA // w + h)),
                  pl.BlockSpec((1, t_len, w), lambda bi, h, qi: (bi, 0, COL_KA // w + h)),
                  pl.BlockSpec((1, t_len, w), lambda bi, h, qi: (bi, 0, COL_VA // w + h))],
        out_specs=pl.BlockSpec((1, t, w), lambda bi, h, qi: (bi, qi, h)),
        out_shape=jax.ShapeDtypeStruct((b, t_len, W_A), F32),
        scratch_shapes=[pltpu.VMEM((2, 2 * t, t), F32),
                        pltpu.VMEM((t_len, w), BF16),
                        pltpu.VMEM((t_len, 2 * w), BF16),
                        pltpu.VMEM((2 * t, w), BF16),
                        pltpu.VMEM((2 * t, LANES), F32),
                        pltpu.VMEM((2 * t, 2 * w), F32)],
        compiler_params=_params(("arbitrary", "arbitrary", "arbitrary")),
        name="diff_attn_prompt",
    )(rel_bias, lamv, subln_g.reshape(1, w), z, z, z)


def _moba_attn_kernel(rel_ref, q_ref, k_ref, v_ref, o_ref,
                      bias_ref, kb_ref, va_ref, kmean_ref, sel_ref, qs_ref, m_ref, acc_ref, *, nb):
    t = ATTN_TILE
    blk_per_tile = t // MOBA_BLOCK
    w = 2 * HD_B
    hp = pl.program_id(1)
    qi = pl.program_id(2)

    @pl.when(qi == 0)
    def _():
        _fill_tile_bias(bias_ref, rel_ref, (2 * H_A + 2 * hp, 2 * H_A + 2 * hp + 1), t)
        _prepare_kv(k_ref, v_ref, kb_ref, va_ref, t)
        kmean_ref[...] = jnp.zeros(kmean_ref.shape, F32)

        def mean_body(n, carry):
            start = pl.multiple_of(n * MOBA_BLOCK, MOBA_BLOCK)
            kmean_ref[pl.ds(n, 1), :] = jnp.mean(k_ref[0, pl.ds(start, MOBA_BLOCK), :], axis=0, keepdims=True)
            return carry

        lax.fori_loop(0, nb, mean_body, 0)

    qf = _stack_streams(q_ref[0], HD_B)
    qs_ref[...] = (qf * (HD_B ** -0.5 * LOG2E)).astype(BF16)
    gate = _dot_nt(qf, kmean_ref[...], precision=lax.Precision.HIGHEST)
    blk = lax.broadcasted_iota(jnp.int32, gate.shape, 1)
    row_in_tile = lax.broadcasted_iota(jnp.int32, gate.shape, 0) % t
    own = qi * blk_per_tile + row_in_tile // MOBA_BLOCK
    sel_ref[...] = jnp.where(_top_k_mask(gate, blk < own, MOBA_TOPK), 1.0, 0.0)
    m_ref[...] = jnp.full(m_ref.shape, NEG, F32)
    acc_ref[...] = jnp.zeros(acc_ref.shape, F32)

    def step(kb, bias, diagonal):
        keys = pl.ds(pl.multiple_of(kb * t, t), t)
        for c in range(2 * t // ATTN_ROW_CHUNK):
            rows = slice(c * ATTN_ROW_CHUNK, (c + 1) * ATTN_ROW_CHUNK)
            own_j = (c * ATTN_ROW_CHUNK % t) // MOBA_BLOCK
            s = _dot_nt(qs_ref[rows, :], kb_ref[keys, :])
            if bias is not None:
                s = s + bias[rows, :]
            sel = sel_ref[rows, :]
            lane = lax.broadcasted_iota(jnp.int32, sel.shape, 1)
            parts = []
            for j in range(blk_per_tile):
                sj = s[:, j * MOBA_BLOCK:(j + 1) * MOBA_BLOCK]
                if diagonal and j == own_j:
                    parts.append(sj)
                else:
                    picked = jnp.sum(jnp.where(lane == kb * blk_per_tile + j, sel, 0.0), axis=-1, keepdims=True)
                    parts.append(jnp.where(picked > 0.5, sj, NEG))
            _flash_step(jnp.concatenate(parts, axis=-1), va_ref[keys, :], m_ref.at[rows, :], acc_ref.at[rows, :])

    def far_body(kb, carry):
        step(kb, None, False)
        return carry

    lax.fori_loop(0, jnp.maximum(qi - 1, 0), far_body, 0)

    @pl.when(qi >= 1)
    def _():
        step(qi - 1, bias_ref[0], False)

    step(qi, bias_ref[1], True)

    acc = acc_ref[...]
    o = acc[:, :w] / acc[:, w:]
    lane = lax.broadcasted_iota(jnp.int32, (t, w), 1)
    o_ref[0] = jnp.where(lane < HD_B, o[:t], o[t:])


def moba_attention_prompt(z, rel_bias):
    b, t_len, _ = z.shape
    t = ATTN_TILE
    assert t_len % t == 0 and t % MOBA_BLOCK == 0 and ATTN_ROW_CHUNK == MOBA_BLOCK
    nb = t_len // MOBA_BLOCK
    assert nb <= LANES
    w = 2 * HD_B
    return pl.pallas_call(
        functools.partial(_moba_attn_kernel, nb=nb),
        grid=(b, H_B // 2, t_len // t),
        in_specs=[pl.BlockSpec(memory_space=pltpu.SMEM),
                  pl.BlockSpec((1, t, w), lambda bi, hp, qi: (bi, qi, COL_QB // w + hp)),
                  pl.BlockSpec((1, t_len, w), lambda bi, hp, qi: (bi, 0, COL_KB // w + hp)),
                  pl.BlockSpec((1, t_len, w), lambda bi, hp, qi: (bi, 0, COL_VB // w + hp))],
        out_specs=pl.BlockSpec((1, t, w), lambda bi, hp, qi: (bi, qi, hp)),
        out_shape=jax.ShapeDtypeStruct((b, t_len, W_B), F32),
        scratch_shapes=[pltpu.VMEM((2, 2 * t, t), F32),
                        pltpu.VMEM((t_len, w), BF16),
                        pltpu.VMEM((t_len, 2 * w), BF16),
                        pltpu.VMEM((LANES, w), F32),
                        pltpu.VMEM((2 * t, LANES), F32),
                        pltpu.VMEM((2 * t, w), BF16),
                        pltpu.VMEM((2 * t, LANES), F32),
                        pltpu.VMEM((2 * t, 2 * w), F32)],
        compiler_params=_params(("arbitrary", "arbitrary", "arbitrary")),
        name="moba_attn_prompt",
    )(rel_bias, z, z, z)


def _masked_query_rows(q, lane_starts, width):
    lane = lax.broadcasted_iota(jnp.int32, q.shape, 1)
    return jnp.concatenate(
        [jnp.where(jnp.logical_and(lane >= s, lane < s + width), q, 0.0) for s in lane_starts], axis=0)


def _group_bias(dist, rel_ref, cols):
    return jnp.concatenate([_rel_bias(dist, rel_ref, col) for col in cols], axis=0)


def _pad_to_page(rows):
    pad = jnp.zeros((PAGE_SIZE - rows.shape[0], rows.shape[1]), rows.dtype)
    return jnp.concatenate([rows, pad], axis=0).astype(BF16)


def _diff_decode_kernel(pt_ref, rel_ref, lamv_ref, g_ref, q_ref, kn_ref, vn_ref, *rest, lam_init):
    del pt_ref
    npg = PAGES_PER_STEP
    kp_refs, vp_refs = rest[:npg], rest[npg:2 * npg]
    o_ref, wq_ref, m_ref, l_ref, acc_ref = rest[2 * npg:]
    c = pl.program_id(1)
    last = pl.num_programs(1) - 1
    r_new = q_ref.shape[1]
    w = 2 * HD_A
    grp = 2 * r_new
    rows = H_A * grp
    flat = PAGE_SIZE * H_A
    cols = [mp * H_A + h for h in range(H_A) for mp in range(2)]

    @pl.when(c == 0)
    def _():
        q = q_ref[0] * (HD_A ** -0.5)
        lane = lax.broadcasted_iota(jnp.int32, (r_new, w), 1)
        parts = []
        for h in range(H_A):
            qh = q[:, h * w:(h + 1) * w]
            parts += [jnp.where(lane < HD_A, qh, 0.0), jnp.where(lane >= HD_A, qh, 0.0)]
        wq_ref[...] = jnp.concatenate(parts, axis=0).astype(BF16)
        m_ref[...] = jnp.full(m_ref.shape, NEG, F32)
        l_ref[...] = jnp.zeros(l_ref.shape, F32)
        acc_ref[...] = jnp.zeros(acc_ref.shape, F32)

    q_head = lax.broadcasted_iota(jnp.int32, (rows, flat), 0) // grp
    k_head = lax.broadcasted_iota(jnp.int32, (rows, flat), 1) % H_A
    head_ok = q_head == k_head

    s_pages = []
    for p in range(npg):
        kf = kp_refs[p][...].reshape(flat, w).astype(BF16)
        s = _dot_nt(wq_ref[...], kf)
        if p == npg - 1:
            r = lax.broadcasted_iota(jnp.int32, (r_new, flat), 0)
            j = lax.broadcasted_iota(jnp.int32, (r_new, flat), 1) // H_A
            dist = jnp.where(c == last, r + PAGE_SIZE - j, FAR_DISTANCE)
            s = s + _group_bias(dist, rel_ref, cols)
        s_pages.append(jnp.where(head_ok, s, NEG))
    m_prev = m_ref[...]
    m_new = m_prev
    for s in s_pages:
        m_new = jnp.maximum(m_new, jnp.max(s, axis=-1, keepdims=True))
    alpha = jnp.exp(m_prev - m_new)
    l_new = alpha * l_ref[...]
    acc = alpha * acc_ref[...]
    for p, s in enumerate(s_pages):
        pexp = jnp.exp(s - m_new)
        l_new = l_new + jnp.sum(pexp, axis=-1, keepdims=True)
        vf = vp_refs[p][...].reshape(flat, w).astype(BF16)
        acc = acc + jnp.dot(pexp.astype(BF16), vf, preferred_element_type=F32)
    m_ref[...] = m_new
    l_ref[...] = l_new
    acc_ref[...] = acc

    @pl.when(c == last)
    def _():
        kn = jnp.concatenate([kn_ref[0, :, h * w:(h + 1) * w] for h in range(H_A)], axis=0)
        vn = jnp.concatenate([vn_ref[0, :, h * w:(h + 1) * w] for h in range(H_A)], axis=0)
        r = lax.broadcasted_iota(jnp.int32, (r_new, PAGE_SIZE), 0)
        j = lax.broadcasted_iota(jnp.int32, (r_new, PAGE_SIZE), 1) % r_new
        s = _dot_nt(wq_ref[...], _pad_to_page(kn)) + _group_bias(r - j, rel_ref, cols)
        qr = lax.broadcasted_iota(jnp.int32, s.shape, 0)
        kc = lax.broadcasted_iota(jnp.int32, s.shape, 1)
        ok = jnp.logical_and(kc // r_new == qr // grp, kc % r_new <= qr % r_new)
        _online_softmax_step(jnp.where(ok, s, NEG), _pad_to_page(vn), m_ref, l_ref, acc_ref)

        o = acc_ref[...] / l_ref[...]
        lv = lamv_ref[...]
        lam = (jnp.exp(jnp.sum(lv[0:1] * lv[1:2], axis=-1, keepdims=True))
               - jnp.exp(jnp.sum(lv[2:3] * lv[3:4], axis=-1, keepdims=True)) + lam_init)
        for h in range(H_A):
            o1 = o[h * grp:h * grp + r_new]
            o2 = o[h * grp + r_new:(h + 1) * grp]
            o_ref[0, :, h * w:(h + 1) * w] = _rms(o1 - lam * o2, g_ref[...]) * (1.0 - lam_init)


def diff_attention_decode(z, cache_k, cache_v, page_table, layer, rel_bias, lamv, subln_g, lam_init):
    n_seq, r_new, _ = z.shape
    n_pages = page_table.shape[1]
    assert n_pages % PAGES_PER_STEP == 0 and r_new == SUBLANES and H_A * r_new <= PAGE_SIZE
    w = 2 * HD_A
    rows = 2 * H_A * r_new
    seq_block = lambda col: pl.BlockSpec((1, r_new, W_A), lambda b, c, pt: (b, 0, col // W_A))
    page_specs = [pl.BlockSpec((None, PAGE_SIZE, None, H_A, w),
                               lambda b, c, pt, p=p: (pt[b, c * PAGES_PER_STEP + p], 0, layer, 0, 0))
                  for p in range(PAGES_PER_STEP)]
    grid_spec = pltpu.PrefetchScalarGridSpec(
        num_scalar_prefetch=1,
        grid=(n_seq, n_pages // PAGES_PER_STEP),
        in_specs=[pl.BlockSpec(memory_space=pltpu.SMEM),
                  pl.BlockSpec((4, HD_A), lambda b, c, pt: (0, 0)),
                  pl.BlockSpec((1, w), lambda b, c, pt: (0, 0)),
                  seq_block(COL_QA), seq_block(COL_KA), seq_block(COL_VA)] + page_specs + page_specs,
        out_specs=pl.BlockSpec((1, r_new, W_A), lambda b, c, pt: (b, 0, 0)),
        scratch_shapes=[pltpu.VMEM((rows, w), BF16),
                        pltpu.VMEM((rows, 1), F32),
                        pltpu.VMEM((rows, 1), F32),
                        pltpu.VMEM((rows, w), F32)])
    return pl.pallas_call(
        functools.partial(_diff_decode_kernel, lam_init=lam_init),
        grid_spec=grid_spec,
        out_shape=jax.ShapeDtypeStruct((n_seq, r_new, W_A), F32),
        compiler_params=_params(("arbitrary", "arbitrary")),
        name="diff_attn_decode",
    )(page_table, rel_bias, lamv, subln_g.reshape(1, w), z, z, z,
      *([cache_k] * PAGES_PER_STEP), *([cache_v] * PAGES_PER_STEP))


def _new_rows_scores(wq, k_new, rel_ref, cols):
    r_new = k_new.shape[0]
    r = lax.broadcasted_iota(jnp.int32, (r_new, PAGE_SIZE), 0)
    j = lax.broadcasted_iota(jnp.int32, (r_new, PAGE_SIZE), 1)
    s = _dot_nt(wq, _pad_to_page(k_new)) + _group_bias(r - j, rel_ref, cols)
    q_row = lax.broadcasted_iota(jnp.int32, s.shape, 0) % r_new
    return jnp.where(lax.broadcasted_iota(jnp.int32, s.shape, 1) <= q_row, s, NEG)


def _moba_decode_kernel(pt_ref, rel_ref, q_ref, kn_ref, vn_ref, *rest):
    del pt_ref
    npg = PAGES_PER_STEP
    kp_refs, vp_refs = rest[:npg], rest[npg:2 * npg]
    o_ref, wq_ref, wqf_ref, kmean_ref, mblk_ref, lblk_ref, accblk_ref = rest[2 * npg:]
    c = pl.program_id(1)
    last = pl.num_programs(1) - 1
    r_new = q_ref.shape[1]
    nb = accblk_ref.shape[0]
    pages_per_block = MOBA_BLOCK // PAGE_SIZE
    blocks_per_step = npg // pages_per_block
    cols = [2 * H_A + h for h in range(H_B)]

    @pl.when(c == 0)
    def _():
        qf = _masked_query_rows(q_ref[0], [h * HD_B for h in range(H_B)], HD_B)
        wqf_ref[...] = qf
        wq_ref[...] = (qf * (HD_B ** -0.5)).astype(BF16)
        kmean_ref[...] = jnp.zeros(kmean_ref.shape, F32)
        mblk_ref[...] = jnp.full(mblk_ref.shape, NEG, F32)
        lblk_ref[...] = jnp.zeros(lblk_ref.shape, F32)

    blk_iota = lax.broadcasted_iota(jnp.int32, mblk_ref.shape, 1)
    mean_lane = lax.broadcasted_iota(jnp.int32, kmean_ref.shape, 1)
    for jb in range(blocks_per_step):
        n = c * blocks_per_step + jb
        pages = range(jb * pages_per_block, (jb + 1) * pages_per_block)
        kts = [kp_refs[p][...].reshape(W_B, PAGE_SIZE) for p in pages]
        s_parts = [jnp.dot(wq_ref[...], kt.astype(BF16), preferred_element_type=F32) for kt in kts]
        if jb == blocks_per_step - 1:
            r = lax.broadcasted_iota(jnp.int32, (r_new, PAGE_SIZE), 0)
            j = lax.broadcasted_iota(jnp.int32, (r_new, PAGE_SIZE), 1)
            dist = jnp.where(c == last, r + PAGE_SIZE - j, FAR_DISTANCE)
            s_parts[-1] = s_parts[-1] + _group_bias(dist, rel_ref, cols)
        s = jnp.concatenate(s_parts, axis=-1)
        m = jnp.max(s, axis=-1, keepdims=True)
        pexp = jnp.exp(s - m)
        acc = sum(_dot_nt(pexp[:, i * PAGE_SIZE:(i + 1) * PAGE_SIZE].astype(BF16),
                          vp_refs[p][...].reshape(W_B, PAGE_SIZE).astype(BF16))
                  for i, p in enumerate(pages))
        accblk_ref[n] = acc
        mblk_ref[...] = jnp.where(blk_iota == n, m, mblk_ref[...])
        lblk_ref[...] = jnp.where(blk_iota == n, jnp.sum(pexp, axis=-1, keepdims=True), lblk_ref[...])
        kmean = sum(jnp.sum(kt, axis=-1, keepdims=True) for kt in kts) * (1.0 / MOBA_BLOCK)
        kmean_ref[...] = jnp.where(mean_lane == n, kmean, kmean_ref[...])

    @pl.when(c == last)
    def _():
        gate = jnp.dot(wqf_ref[...], kmean_ref[...], precision=lax.Precision.HIGHEST,
                       preferred_element_type=F32)
        sel = _top_k_mask(gate, blk_iota < nb, MOBA_TOPK)
        s_own = _new_rows_scores(wq_ref[...], kn_ref[0], rel_ref, cols)
        mblk = jnp.where(sel, mblk_ref[...], NEG)
        m_all = jnp.maximum(jnp.max(mblk, axis=-1, keepdims=True), jnp.max(s_own, axis=-1, keepdims=True))
        wgt = jnp.where(sel, jnp.exp(mblk - m_all), 0.0)
        p_own = jnp.exp(s_own - m_all)
        denom = (jnp.sum(wgt * lblk_ref[...], axis=-1, keepdims=True)
                 + jnp.sum(p_own, axis=-1, keepdims=True))
        acc = jnp.dot(p_own.astype(BF16), _pad_to_page(vn_ref[0]), preferred_element_type=F32)
        for n in range(nb):
            acc = acc + wgt[:, n:n + 1] * accblk_ref[n]
        o = acc / denom
        lane = lax.broadcasted_iota(jnp.int32, (r_new, W_B), 1)
        out = jnp.zeros((r_new, W_B), F32)
        for h in range(H_B):
            in_head = jnp.logical_and(lane >= h * HD_B, lane < (h + 1) * HD_B)
            out = out + jnp.where(in_head, o[h * r_new:(h + 1) * r_new, :], 0.0)
        o_ref[0] = out


def moba_attention_decode(z, cache_kt, cache_vt, page_table, layer, rel_bias):
    n_seq, r_new, _ = z.shape
    n_pages = page_table.shape[1]
    past = n_pages * PAGE_SIZE
    assert n_pages % PAGES_PER_STEP == 0 and past % MOBA_BLOCK == 0 and r_new == SUBLANES
    assert PAGES_PER_STEP % (MOBA_BLOCK // PAGE_SIZE) == 0 and r_new <= MOBA_BLOCK
    nb = past // MOBA_BLOCK
    assert nb <= LANES
    rows = H_B * r_new
    seq_block = lambda col: pl.BlockSpec((1, r_new, W_B), lambda b, c, pt: (b, 0, col // W_B))
    page_specs = [pl.BlockSpec((None, None, H_B, HD_B, PAGE_SIZE),
                               lambda b, c, pt, p=p: (pt[b, c * PAGES_PER_STEP + p], layer, 0, 0, 0))
                  for p in range(PAGES_PER_STEP)]
    grid_spec = pltpu.PrefetchScalarGridSpec(
        num_scalar_prefetch=1,
        grid=(n_seq, n_pages // PAGES_PER_STEP),
        in_specs=[pl.BlockSpec(memory_space=pltpu.SMEM),
                  seq_block(COL_QB), seq_block(COL_KB), seq_block(COL_VB)] + page_specs + page_specs,
        out_specs=pl.BlockSpec((1, r_new, W_B), lambda b, c, pt: (b, 0, 0)),
        scratch_shapes=[pltpu.VMEM((rows, W_B), BF16),
                        pltpu.VMEM((rows, W_B), F32),
                        pltpu.VMEM((W_B, LANES), F32),
                        pltpu.VMEM((rows, LANES), F32),
                        pltpu.VMEM((rows, LANES), F32),
                        pltpu.VMEM((nb, rows, W_B), F32)])
    return pl.pallas_call(
        _moba_decode_kernel,
        grid_spec=grid_spec,
        out_shape=jax.ShapeDtypeStruct((n_seq, r_new, W_B), F32),
        compiler_params=_params(("arbitrary", "arbitrary")),
        name="moba_attn_decode",
    )(page_table, rel_bias, z, z, z, *([cache_kt] * PAGES_PER_STEP), *([cache_vt] * PAGES_PER_STEP))


def _sigmoid(x):
    return 1.0 / (1.0 + jnp.exp(-x))


def _mix_out_kernel(oa_ref, ob_ref, ga_ref, gb_ref, x_ref, wa_ref, wb_ref, wo_ref, o_ref):
    ya = jnp.dot(oa_ref[...].astype(BF16), wa_ref[...], preferred_element_type=F32)
    yb = jnp.dot(ob_ref[...].astype(BF16), wb_ref[...], preferred_element_type=F32)
    mixed = _sigmoid(ga_ref[...]) * ya + _sigmoid(gb_ref[...]) * yb
    o_ref[...] = x_ref[...] + jnp.dot(mixed.astype(BF16), wo_ref[...], preferred_element_type=F32)


def mix_out(oa, ob, z, x, wa, wb, wo):
    n, d = x.shape
    tm = min(MIX_ROW_TILE, n)
    assert n % tm == 0
    full = lambda a: pl.BlockSpec(a.shape, lambda i: (0, 0))
    return pl.pallas_call(
        _mix_out_kernel,
        grid=(n // tm,),
        in_specs=[pl.BlockSpec((tm, W_A), lambda i: (i, 0)),
                  pl.BlockSpec((tm, W_B), lambda i: (i, 0)),
                  pl.BlockSpec((tm, d), lambda i: (i, COL_GA // d)),
                  pl.BlockSpec((tm, d), lambda i: (i, COL_GB // d)),
                  pl.BlockSpec((tm, d), lambda i: (i, 0)),
                  full(wa), full(wb), full(wo)],
        out_specs=pl.BlockSpec((tm, d), lambda i: (i, 0)),
        out_shape=jax.ShapeDtypeStruct((n, d), F32),
        compiler_params=_params(("arbitrary",)),
        name="mix_out",
    )(oa, ob, z, z, x, wa, wb, wo)


def _mem_attn_kernel(x_ref, g_ref, k_ref, v_ref, wq_ref, wo_ref, o_ref):
    x = x_ref[0]
    q = jnp.dot(_rms(x, g_ref[...]).astype(BF16), wq_ref[...], preferred_element_type=F32)
    heads = []
    for h in range(H_M):
        sl = slice(h * HD_M, (h + 1) * HD_M)
        s = _dot_nt((q[:, sl] * (HD_M ** -0.5)).astype(BF16), k_ref[0, :, sl].astype(BF16))
        p = jnp.exp(s - jnp.max(s, axis=-1, keepdims=True))
        oh = jnp.dot(p.astype(BF16), v_ref[0, :, sl].astype(BF16), preferred_element_type=F32)
        heads.append(oh / jnp.sum(p, axis=-1, keepdims=True))
    o = jnp.concatenate(heads, axis=-1)
    o_ref[0] = x + jnp.dot(o.astype(BF16), wo_ref[...], preferred_element_type=F32)


def memory_attention(x, g, mem_k, k_col, mem_v, v_col, wq, wo):
    grp, t_len, d = x.shape
    n_mem = mem_k.shape[1]
    tm = min(MIX_ROW_TILE, t_len)
    assert t_len % tm == 0
    return pl.pallas_call(
        _mem_attn_kernel,
        grid=(grp, t_len // tm),
        in_specs=[pl.BlockSpec((1, tm, d), lambda b, i: (b, i, 0)),
                  pl.BlockSpec((1, d), lambda b, i: (0, 0)),
                  pl.BlockSpec((1, n_mem, W_M), lambda b, i: (b, 0, k_col)),
                  pl.BlockSpec((1, n_mem, W_M), lambda b, i: (b, 0, v_col)),
                  pl.BlockSpec(wq.shape, lambda b, i: (0, 0)),
                  pl.BlockSpec(wo.shape, lambda b, i: (0, 0))],
        out_specs=pl.BlockSpec((1, tm, d), lambda b, i: (b, i, 0)),
        out_shape=jax.ShapeDtypeStruct((grp, t_len, d), F32),
        compiler_params=_params(("arbitrary", "arbitrary")),
        name="memory_attention",
    )(x, g.reshape(1, d), mem_k, mem_v, wq, wo)


def _swiglu_hidden(xn, wg, wu):
    a = jnp.dot(xn, wg, preferred_element_type=F32)
    return a * _sigmoid(a) * jnp.dot(xn, wu, preferred_element_type=F32)


def _dense_ffn_kernel(x_ref, g_ref, wg_ref, wu_ref, wd_ref, o_ref, xn_ref, acc_ref):
    j = pl.program_id(1)

    @pl.when(j == 0)
    def _():
        xn_ref[...] = _rms(x_ref[...], g_ref[...]).astype(BF16)
        acc_ref[...] = jnp.zeros(acc_ref.shape, F32)

    hid = _swiglu_hidden(xn_ref[...], wg_ref[...], wu_ref[...])
    acc_ref[...] += jnp.dot(hid.astype(BF16), wd_ref[...], preferred_element_type=F32)

    @pl.when(j == pl.num_programs(1) - 1)
    def _():
        o_ref[...] = x_ref[...] + acc_ref[...]


def dense_ffn(x, g, wg, wu, wd):
    n, d = x.shape
    f = wg.shape[1]
    tm = min(ROW_TILE, n)
    tf = FF_TILE
    assert n % tm == 0 and f % tf == 0
    return pl.pallas_call(
        _dense_ffn_kernel,
        grid=(n // tm, f // tf),
        in_specs=[pl.BlockSpec((tm, d), lambda i, j: (i, 0)),
                  pl.BlockSpec((1, d), lambda i, j: (0, 0)),
                  pl.BlockSpec((d, tf), lambda i, j: (0, j)),
                  pl.BlockSpec((d, tf), lambda i, j: (0, j)),
                  pl.BlockSpec((tf, d), lambda i, j: (j, 0))],
        out_specs=pl.BlockSpec((tm, d), lambda i, j: (i, 0)),
        out_shape=jax.ShapeDtypeStruct((n, d), F32),
        scratch_shapes=[pltpu.VMEM((tm, d), BF16), pltpu.VMEM((tm, d), F32)],
        compiler_params=_params(("arbitrary", "arbitrary")),
        name="dense_ffn",
    )(x, g.reshape(1, d), wg, wu, wd)


def _moe_ffn_kernel(x_ref, g_ref, r_ref, wg_ref, wu_ref, wd_ref, o_ref, xn_ref, gate_ref, acc_ref):
    e = pl.program_id(1)
    j = pl.program_id(2)

    @pl.when(jnp.logical_and(e == 0, j == 0))
    def _():
        xn = _rms(x_ref[...], g_ref[...])
        xn_ref[...] = xn.astype(BF16)
        acc_ref[...] = jnp.zeros(acc_ref.shape, F32)
        logits = jnp.dot(xn, r_ref[...], precision=lax.Precision.HIGHEST, preferred_element_type=F32)
        lane = lax.broadcasted_iota(jnp.int32, logits.shape, 1).astype(F32)
        logits = jnp.where(lane < N_EXPERTS, logits, -jnp.inf)
        v1 = jnp.max(logits, axis=-1, keepdims=True)
        i1 = jnp.min(jnp.where(logits == v1, lane, float(LANES)), axis=-1, keepdims=True)
        rest = jnp.where(lane == i1, -jnp.inf, logits)
        v2 = jnp.max(rest, axis=-1, keepdims=True)
        i2 = jnp.min(jnp.where(rest == v2, lane, float(LANES)), axis=-1, keepdims=True)
        e2 = jnp.exp(v2 - v1)
        denom = 1.0 + e2
        gate_ref[...] = jnp.where(lane == i1, 1.0 / denom, 0.0) + jnp.where(lane == i2, e2 / denom, 0.0)

    gates = gate_ref[...]
    lane = lax.broadcasted_iota(jnp.int32, gates.shape, 1)
    ge = jnp.sum(jnp.where(lane == e, gates, 0.0), axis=-1, keepdims=True)
    hid = _swiglu_hidden(xn_ref[...], wg_ref[0], wu_ref[0]) * ge
    acc_ref[...] += jnp.dot(hid.astype(BF16), wd_ref[0], preferred_element_type=F32)

    @pl.when(jnp.logical_and(e == pl.num_programs(1) - 1, j == pl.num_programs(2) - 1))
    def _():
        o_ref[...] = x_ref[...] + acc_ref[...]


def moe_ffn(x, g, router_padded, wg, wu, wd):
    n, d = x.shape
    n_exp, _, f = wg.shape
    tm = min(ROW_TILE, n)
    tf = FF_TILE
    assert n % tm == 0 and f % tf == 0 and n_exp == N_EXPERTS
    return pl.pallas_call(
        _moe_ffn_kernel,
        grid=(n // tm, n_exp, f // tf),
        in_specs=[pl.BlockSpec((tm, d), lambda i, e, j: (i, 0)),
                  pl.BlockSpec((1, d), lambda i, e, j: (0, 0)),
                  pl.BlockSpec((d, LANES), lambda i, e, j: (0, 0)),
                  pl.BlockSpec((1, d, tf), lambda i, e, j: (e, 0, j)),
                  pl.BlockSpec((1, d, tf), lambda i, e, j: (e, 0, j)),
                  pl.BlockSpec((1, tf, d), lambda i, e, j: (e, j, 0))],
        out_specs=pl.BlockSpec((tm, d), lambda i, e, j: (i, 0)),
        out_shape=jax.ShapeDtypeStruct((n, d), F32),
        scratch_shapes=[pltpu.VMEM((tm, d), BF16), pltpu.VMEM((tm, LANES), F32), pltpu.VMEM((tm, d), F32)],
        compiler_params=_params(("arbitrary", "arbitrary", "arbitrary")),
        name="moe_ffn",
    )(x, g.reshape(1, d), router_padded, wg, wu, wd)


def kernel(x_prompt, x_sample, mem_prompt, cache_a_k, cache_a_v, cache_b_k, cache_b_v, cache_mem_k, cache_mem_v, page_table, rel_bias, norm_mix, w_in, lambda_q1, lambda_k1, lambda_q2, lambda_k2, subln_gain, w_br_a, w_br_b, w_out, norm_mem_q, norm_mem_kv, w_mem_q, w_mem_kv, w_mem_o, norm_ffn, ffn_w_gate, ffn_w_up, ffn_w_down, moe_router, moe_w_gate, moe_w_up, moe_w_down, norm_final):
    bsz, seq, d = x_prompt.shape
    n_seq, r_new, _ = x_sample.shape
    n_mem = mem_prompt.shape[1]
    depth = w_in.shape[0]

    cb_kt = jnp.transpose(cache_b_k, (0, 2, 3, 4, 1))
    cb_vt = jnp.transpose(cache_b_v, (0, 2, 3, 4, 1))
    cm_k = cache_mem_k.reshape(n_seq, n_mem, depth * W_M)
    cm_v = cache_mem_v.reshape(n_seq, n_mem, depth * W_M)

    hp = x_prompt.reshape(bsz * seq, d)
    hs = x_sample.reshape(n_seq * r_new, d)
    zs_p, zs_s, mkvs = [], [], []
    for l in range(depth):
        lam_init = 0.8 - 0.6 * math.exp(-0.3 * l)
        lamv = jnp.stack([lambda_q1[l], lambda_k1[l], lambda_q2[l], lambda_k2[l]]).astype(F32)
        w_in_l = w_in[l].astype(BF16)
        wa, wb, wo = w_br_a[l].astype(BF16), w_br_b[l].astype(BF16), w_out[l].astype(BF16)

        zp = norm_matmul(hp, norm_mix[l], w_in_l)
        zp3 = zp.reshape(bsz, seq, IN_COLS)
        oa = diff_attention_prompt(zp3, rel_bias, lamv, subln_gain[l], lam_init)
        ob = moba_attention_prompt(zp3, rel_bias)
        hp = mix_out(oa.reshape(bsz * seq, W_A), ob.reshape(bsz * seq, W_B), zp, hp, wa, wb, wo)
        zs_p.append(zp3)

        zs = norm_matmul(hs, norm_mix[l], w_in_l)
        zs3 = zs.reshape(n_seq, r_new, IN_COLS)
        oa_s = diff_attention_decode(zs3, cache_a_k, cache_a_v, page_table, l, rel_bias, lamv, subln_gain[l], lam_init)
        ob_s = moba_attention_decode(zs3, cb_kt, cb_vt, page_table, l, rel_bias)
        hs = mix_out(oa_s.reshape(n_seq * r_new, W_A), ob_s.reshape(n_seq * r_new, W_B), zs, hs, wa, wb, wo)
        zs_s.append(zs3)

        mkv = norm_matmul(mem_prompt.reshape(bsz * n_mem, d), norm_mem_kv[l], w_mem_kv[l].astype(BF16))
        mkv3 = mkv.reshape(bsz, n_mem, 2 * W_M)
        mkvs.append(mkv3)
        wq, wmo = w_mem_q[l].astype(BF16), w_mem_o[l].astype(BF16)
        hp = memory_attention(hp.reshape(bsz, seq, d), norm_mem_q[l], mkv3, 0, mkv3, 1, wq, wmo).reshape(bsz * seq, d)
        hs = memory_attention(hs.reshape(n_seq, r_new, d), norm_mem_q[l], cm_k, l, cm_v, l, wq, wmo).reshape(n_seq * r_new, d)

        i = l // 2
        if l % 2 == 0:
            wg, wu, wd = ffn_w_gate[i].astype(BF16), ffn_w_up[i].astype(BF16), ffn_w_down[i].astype(BF16)
            hp = dense_ffn(hp, norm_ffn[l], wg, wu, wd)
            hs = dense_ffn(hs, norm_ffn[l], wg, wu, wd)
        else:
            wg, wu, wd = moe_w_gate[i].astype(BF16), moe_w_up[i].astype(BF16), moe_w_down[i].astype(BF16)
            router = jnp.pad(moe_router[i].astype(F32), ((0, 0), (0, LANES - N_EXPERTS)))
            hp = moe_ffn(hp, norm_ffn[l], router, wg, wu, wd)
            hs = moe_ffn(hs, norm_ffn[l], router, wg, wu, wd)

    y_prompt = rmsnorm(hp, norm_final).reshape(bsz, seq, d)
    y_sample = rmsnorm(hs, norm_final).reshape(n_seq, r_new, d)

    def stacked(zs, col, width, heads):
        rows = jnp.stack([z[:, :, col:col + width] for z in zs], axis=2)
        return rows.reshape(rows.shape[0], rows.shape[1], depth, heads, width // heads)

    new_mem_k = jnp.stack([m[:, :, :W_M] for m in mkvs], axis=2).reshape(bsz, n_mem, depth, H_M, HD_M)
    new_mem_v = jnp.stack([m[:, :, W_M:] for m in mkvs], axis=2).reshape(bsz, n_mem, depth, H_M, HD_M)
    return (y_prompt, y_sample,
            stacked(zs_p, COL_KA, W_A, H_A), stacked(zs_p, COL_VA, W_A, H_A),
            stacked(zs_p, COL_KB, W_B, H_B), stacked(zs_p, COL_VB, W_B, H_B),
            new_mem_k, new_mem_v,
            stacked(zs_s, COL_KA, W_A, H_A), stacked(zs_s, COL_VA, W_A, H_A),
            stacked(zs_s, COL_KB, W_B, H_B), stacked(zs_s, COL_VB, W_B, H_B))
```

```python
import functools
import math

import numpy as np
import jax
import jax.numpy as jnp
from jax import lax
from jax.experimental import pallas as pl
from jax.experimental.pallas import tpu as pltpu

F32 = jnp.float32
BF16 = jnp.bfloat16

D_MODEL = 1024
DEPTH = 2
PAGE_SIZE = 128
H_A = 4
HD_A = 64
W_A = H_A * 2 * HD_A
H_B = 8
HD_B = 64
W_B = H_B * HD_B
MOBA_BLOCK = 256
MOBA_TOPK = 3
H_M = 4
HD_M = 128
W_M = H_M * HD_M
N_BUCKETS = 32
MAX_DISTANCE = 128
N_EXPERTS = 8
TOP_K_EXPERTS = 2
IN_COLS = 3 * W_A + 3 * W_B + 2 * D_MODEL
EPS = 1e-6
NEG = -1e30
LOG2E = math.log2(math.e)

LANES = 128
SUBLANES = 8
VMEM_LIMIT_BYTES = 56 * 1024 * 1024

ATTN_TILE = 2 * MOBA_BLOCK
ATTN_ROW_CHUNK = MOBA_BLOCK
ATTN_UNROLL = 4
PAGES_PER_STEP = 8
ROW_TILE = 1024
COL_TILE = 1024
FF_TILE = 256
MOE_CHUNK = 320
MIX_ROW_TILE = 512

COL_QA, COL_KA, COL_VA = 0, W_A, 2 * W_A
COL_QB, COL_KB, COL_VB = 3 * W_A, 3 * W_A + W_B, 3 * W_A + 2 * W_B
COL_GA = 3 * W_A + 3 * W_B
COL_GB = COL_GA + D_MODEL


def _t5_bucket_starts():
    n = np.arange(0, 4 * MAX_DISTANCE, dtype=np.int64)
    max_exact = N_BUCKETS // 2
    nf = np.maximum(n, 1).astype(np.float32)
    large = max_exact + (np.log(nf / np.float32(max_exact))
                         / np.float32(math.log(MAX_DISTANCE / max_exact))
                         * np.float32(N_BUCKETS - max_exact)).astype(np.int32)
    bucket = np.where(n < max_exact, n, np.minimum(large, N_BUCKETS - 1))
    assert np.all(np.diff(bucket) >= 0) and bucket[-1] == N_BUCKETS - 1
    starts = [int(np.argmax(bucket >= b)) for b in range(N_BUCKETS)]
    assert starts[-1] <= MAX_DISTANCE
    return starts


BUCKET_START = _t5_bucket_starts()
FAR_DISTANCE = BUCKET_START[-1]


def _params(semantics):
    return pltpu.CompilerParams(dimension_semantics=semantics, vmem_limit_bytes=VMEM_LIMIT_BYTES)


def _rms(x, g):
    return x * lax.rsqrt(jnp.mean(x * x, axis=-1, keepdims=True) + EPS) * g


def _rel_bias(dist, rel_ref, col, scale=1.0):
    last = rel_ref[N_BUCKETS - 1, col]
    out = jnp.zeros(dist.shape, F32)
    for b in range(N_BUCKETS - 2, -1, -1):
        out = jnp.where(dist < BUCKET_START[b + 1], (rel_ref[b, col] - last) * scale, out)
    return out


def _dot_nt(a, b, precision=None):
    return lax.dot_general(a, b, (((1,), (1,)), ((), ())), precision=precision,
                           preferred_element_type=F32)


def _online_softmax_step(s, v, m_ref, l_ref, acc_ref):
    m_prev = m_ref[...]
    m_new = jnp.maximum(m_prev, jnp.max(s, axis=-1, keepdims=True))
    alpha = jnp.exp(m_prev - m_new)
    p = jnp.exp(s - m_new)
    l_ref[...] = alpha * l_ref[...] + jnp.sum(p, axis=-1, keepdims=True)
    acc_ref[...] = alpha * acc_ref[...] + jnp.dot(p.astype(BF16), v, preferred_element_type=F32)
    m_ref[...] = m_new


def _top_k_mask(gate, valid, k):
    idx = lax.broadcasted_iota(jnp.int32, gate.shape, 1).astype(F32)
    g = jnp.where(valid, gate, NEG)
    sel = jnp.zeros(gate.shape, F32)
    for _ in range(k):
        mx = jnp.max(g, axis=-1, keepdims=True)
        first = jnp.min(jnp.where(g == mx, idx, float(gate.shape[-1])), axis=-1, keepdims=True)
        pick = idx == first
        sel = jnp.where(pick, 1.0, sel)
        g = jnp.where(pick, -jnp.inf, g)
    return jnp.logical_and(sel > 0.5, valid)


def _norm_matmul_kernel(x_ref, g_ref, w_ref, o_ref, xn_ref):
    @pl.when(pl.program_id(1) == 0)
    def _():
        xn_ref[...] = _rms(x_ref[...], g_ref[...]).astype(BF16)

    o_ref[...] = jnp.dot(xn_ref[...], w_ref[...], preferred_element_type=F32)


def norm_matmul(x, g, w_bf16):
    n, d = x.shape
    c = w_bf16.shape[1]
    tm = min(ROW_TILE, n)
    tn = min(COL_TILE, c)
    assert n % tm == 0 and c % tn == 0
    return pl.pallas_call(
        _norm_matmul_kernel,
        grid=(n // tm, c // tn),
        in_specs=[pl.BlockSpec((tm, d), lambda i, j: (i, 0)),
                  pl.BlockSpec((1, d), lambda i, j: (0, 0)),
                  pl.BlockSpec((d, tn), lambda i, j: (0, j))],
        out_specs=pl.BlockSpec((tm, tn), lambda i, j: (i, j)),
        out_shape=jax.ShapeDtypeStruct((n, c), F32),
        scratch_shapes=[pltpu.VMEM((tm, d), BF16)],
        compiler_params=_params(("arbitrary", "arbitrary")),
        name="norm_matmul",
    )(x, g.reshape(1, d), w_bf16)


def _rmsnorm_kernel(x_ref, g_ref, o_ref):
    o_ref[...] = _rms(x_ref[...], g_ref[...])


def rmsnorm(x, g):
    n, d = x.shape
    tm = min(ROW_TILE, n)
    return pl.pallas_call(
        _rmsnorm_kernel,
        grid=(n // tm,),
        in_specs=[pl.BlockSpec((tm, d), lambda i: (i, 0)), pl.BlockSpec((1, d), lambda i: (0, 0))],
        out_specs=pl.BlockSpec((tm, d), lambda i: (i, 0)),
        out_shape=jax.ShapeDtypeStruct((n, d), F32),
        compiler_params=_params(("arbitrary",)),
        name="final_rmsnorm",
    )(x, g.reshape(1, d))


def _stack_streams(q, half):
    lane = lax.broadcasted_iota(jnp.int32, q.shape, 1)
    return jnp.concatenate([jnp.where(lane < half, q, 0.0), jnp.where(lane >= half, q, 0.0)], axis=0)


def _fill_tile_bias(bias_ref, rel_ref, cols, t):
    r = lax.broadcasted_iota(jnp.int32, (t, t), 0)
    c = lax.broadcasted_iota(jnp.int32, (t, t), 1)
    d_diag = r - c
    for s, col in enumerate(cols):
        bias_ref[0, s * t:(s + 1) * t, :] = _rel_bias(d_diag + t, rel_ref, col, LOG2E)
        bias_ref[1, s * t:(s + 1) * t, :] = jnp.where(d_diag >= 0, _rel_bias(d_diag, rel_ref, col, LOG2E), NEG)


def _prepare_kv(k_ref, v_ref, kb_ref, va_ref, t):
    n_blocks = k_ref.shape[1] // t
    w = v_ref.shape[2]

    def body(n, carry):
        rows = pl.ds(pl.multiple_of(n * t, t), t)
        kb_ref[rows, :] = k_ref[0, rows, :].astype(BF16)
        va_ref[rows, :w] = v_ref[0, rows, :].astype(BF16)
        va_ref[rows, w:] = jnp.ones((t, w), BF16)
        return carry

    lax.fori_loop(0, n_blocks, body, 0)


def _flash_step(s, va, m_ref, acc_ref):
    m_prev = m_ref[...]
    m_new = jnp.maximum(m_prev, jnp.max(s, axis=-1, keepdims=True))
    alpha = jnp.exp2(m_prev - m_new)
    p = jnp.exp2(s - jnp.concatenate([m_new] * (s.shape[1] // LANES), axis=1))
    acc_ref[...] = (jnp.concatenate([alpha] * (acc_ref.shape[1] // LANES), axis=1) * acc_ref[...]
                    + jnp.dot(p.astype(BF16), va, preferred_element_type=F32))
    m_ref[...] = m_new


def _far_blocks(n_far, step):
    def group(g, carry):
        for u in range(ATTN_UNROLL):
            step(ATTN_UNROLL * g + u)
        return carry

    def single(kb, carry):
        step(kb)
        return carry

    lax.fori_loop(0, n_far // ATTN_UNROLL, group, 0)
    lax.fori_loop(n_far - n_far % ATTN_UNROLL, n_far, single, 0)


def _diff_attn_kernel(rel_ref, lamv_ref, g_ref, q_ref, k_ref, v_ref, o_ref,
                      bias_ref, kb_ref, va_ref, qs_ref, m_ref, acc_ref, *, lam_init):
    t = ATTN_TILE
    w = 2 * HD_A
    h = pl.program_id(1)
    qi = pl.program_id(2)

    @pl.when(qi == 0)
    def _():
        _fill_tile_bias(bias_ref, rel_ref, (h, H_A + h), t)
        _prepare_kv(k_ref, v_ref, kb_ref, va_ref, t)

    qs_ref[...] = _stack_streams(q_ref[0] * (HD_A ** -0.5 * LOG2E), HD_A).astype(BF16)
    m_ref[...] = jnp.full(m_ref.shape, NEG, F32)
    acc_ref[...] = jnp.zeros(acc_ref.shape, F32)

    def step(kb, bias):
        keys = pl.ds(pl.multiple_of(kb * t, t), t)
        for c in range(2 * t // ATTN_ROW_CHUNK):
            rows = slice(c * ATTN_ROW_CHUNK, (c + 1) * ATTN_ROW_CHUNK)
            s = _dot_nt(qs_ref[rows, :], kb_ref[keys, :])
            if bias is not None:
                s = s + bias[rows, :]
            _flash_step(s, va_ref[keys, :], m_ref.at[rows, :], acc_ref.at[rows, :])

    _far_blocks(jnp.maximum(qi - 1, 0), lambda kb: step(kb, None))

    @pl.when(qi >= 1)
    def _():
        step(qi - 1, bias_ref[0])

    step(qi, bias_ref[1])

    acc = acc_ref[...]
    o = acc[:, :w] / acc[:, w:]
    lv = lamv_ref[...]
    lam = (jnp.exp(jnp.sum(lv[0:1] * lv[1:2], axis=-1, keepdims=True))
           - jnp.exp(jnp.sum(lv[2:3] * lv[3:4], axis=-1, keepdims=True)) + lam_init)
    d = o[:t] - lam * o[t:]
    o_ref[0] = _rms(d, g_ref[...]) * (1.0 - lam_init)


def diff_attention_prompt(z, rel_bias, lamv, subln_g, lam_init):
    b, t_len, _ = z.shape
    t = ATTN_TILE
    assert t_len % t == 0
    w = 2 * HD_A
    return pl.pallas_call(
        functools.partial(_diff_attn_kernel, lam_init=lam_init),
        grid=(b, H_A, t_len // t),
        in_specs=[pl.BlockSpec(memory_space=pltpu.SMEM),
                  pl.BlockSpec((4, HD_A), lambda bi, h, qi: (0, 0)),
                  pl.BlockSpec((1, w), lambda bi, h, qi: (0, 0)),
                  pl.BlockSpec((1, t, w), lambda bi, h, qi: (bi, qi, COL_QA // w + h)),
                  pl.BlockSpec((1, t_len, w), lambda bi, h, qi: (bi, 0, COL_KA // w + h)),
                  pl.BlockSpec((1, t_len, w), lambda bi, h, qi: (bi, 0, COL_VA // w + h))],
        out_specs=pl.BlockSpec((1, t, w), lambda bi, h, qi: (bi, qi, h)),
        out_shape=jax.ShapeDtypeStruct((b, t_len, W_A), F32),
        scratch_shapes=[pltpu.VMEM((2, 2 * t, t), F32),
                        pltpu.VMEM((t_len, w), BF16),
                        pltpu.VMEM((t_len, 2 * w), BF16),
                        pltpu.VMEM((2 * t, w), BF16),
                        pltpu.VMEM((2 * t, LANES), F32),
                        pltpu.VMEM((2 * t, 2 * w), F32)],
        compiler_params=_params(("arbitrary", "arbitrary", "arbitrary")),
        name="diff_attn_prompt",
    )(rel_bias, lamv, subln_g.reshape(1, w), z, z, z)


def _moba_attn_kernel(rel_ref, q_ref, k_ref, v_ref, o_ref,
                      bias_ref, kb_ref, va_ref, kmean_ref, sel_ref, qs_ref, m_ref, acc_ref, *, nb):
    t = ATTN_TILE
    blk_per_tile = t // MOBA_BLOCK
    w = 2 * HD_B
    hp = pl.program_id(1)
    qi = pl.program_id(2)

    @pl.when(qi == 0)
    def _():
        _fill_tile_bias(bias_ref, rel_ref, (2 * H_A + 2 * hp, 2 * H_A + 2 * hp + 1), t)
        _prepare_kv(k_ref, v_ref, kb_ref, va_ref, t)
        kmean_ref[...] = jnp.zeros(kmean_ref.shape, F32)

        def mean_body(n, carry):
            start = pl.multiple_of(n * MOBA_BLOCK, MOBA_BLOCK)
            kmean_ref[pl.ds(n, 1), :] = jnp.mean(k_ref[0, pl.ds(start, MOBA_BLOCK), :], axis=0, keepdims=True)
            return carry

        lax.fori_loop(0, nb, mean_body, 0)

    qf = _stack_streams(q_ref[0], HD_B)
    qs_ref[...] = (qf * (HD_B ** -0.5 * LOG2E)).astype(BF16)
    gate = _dot_nt(qf, kmean_ref[...], precision=lax.Precision.HIGHEST)
    blk = lax.broadcasted_iota(jnp.int32, gate.shape, 1)
    row_in_tile = lax.broadcasted_iota(jnp.int32, gate.shape, 0) % t
    own = qi * blk_per_tile + row_in_tile // MOBA_BLOCK
    sel_ref[...] = jnp.where(_top_k_mask(gate, blk < own, MOBA_TOPK), 1.0, 0.0)
    m_ref[...] = jnp.full(m_ref.shape, NEG, F32)
    acc_ref[...] = jnp.zeros(acc_ref.shape, F32)

    def step(kb, bias, diagonal):
        keys = pl.ds(pl.multiple_of(kb * t, t), t)
        for c in range(2 * t // ATTN_ROW_CHUNK):
            rows = slice(c * ATTN_ROW_CHUNK, (c + 1) * ATTN_ROW_CHUNK)
            own_j = (c * ATTN_ROW_CHUNK % t) // MOBA_BLOCK
            s = _dot_nt(qs_ref[rows, :], kb_ref[keys, :])
            if bias is not None:
                s = s + bias[rows, :]
            sel = sel_ref[rows, :]
            lane = lax.broadcasted_iota(jnp.int32, sel.shape, 1)
            parts = []
            for j in range(blk_per_tile):
                sj = s[:, j * MOBA_BLOCK:(j + 1) * MOBA_BLOCK]
                if diagonal and j == own_j:
                    parts.append(sj)
                else:
                    picked = jnp.sum(jnp.where(lane == kb * blk_per_tile + j, sel, 0.0), axis=-1, keepdims=True)
                    parts.append(jnp.where(picked > 0.5, sj, NEG))
            _flash_step(jnp.concatenate(parts, axis=-1), va_ref[keys, :], m_ref.at[rows, :], acc_ref.at[rows, :])

    _far_blocks(jnp.maximum(qi - 1, 0), lambda kb: step(kb, None, False))

    @pl.when(qi >= 1)
    def _():
        step(qi - 1, bias_ref[0], False)

    step(qi, bias_ref[1], True)

    acc = acc_ref[...]
    o = acc[:, :w] / acc[:, w:]
    lane = lax.broadcasted_iota(jnp.int32, (t, w), 1)
    o_ref[0] = jnp.where(lane < HD_B, o[:t], o[t:])


def moba_attention_prompt(z, rel_bias):
    b, t_len, _ = z.shape
    t = ATTN_TILE
    assert t_len % t == 0 and t % MOBA_BLOCK == 0 and ATTN_ROW_CHUNK == MOBA_BLOCK
    nb = t_len // MOBA_BLOCK
    assert nb <= LANES
    w = 2 * HD_B
    return pl.pallas_call(
        functools.partial(_moba_attn_kernel, nb=nb),
        grid=(b, H_B // 2, t_len // t),
        in_specs=[pl.BlockSpec(memory_space=pltpu.SMEM),
                  pl.BlockSpec((1, t, w), lambda bi, hp, qi: (bi, qi, COL_QB // w + hp)),
                  pl.BlockSpec((1, t_len, w), lambda bi, hp, qi: (bi, 0, COL_KB // w + hp)),
                  pl.BlockSpec((1, t_len, w), lambda bi, hp, qi: (bi, 0, COL_VB // w + hp))],
        out_specs=pl.BlockSpec((1, t, w), lambda bi, hp, qi: (bi, qi, hp)),
        out_shape=jax.ShapeDtypeStruct((b, t_len, W_B), F32),
        scratch_shapes=[pltpu.VMEM((2, 2 * t, t), F32),
                        pltpu.VMEM((t_len, w), BF16),
                        pltpu.VMEM((t_len, 2 * w), BF16),
                        pltpu.VMEM((LANES, w), F32),
                        pltpu.VMEM((2 * t, LANES), F32),
                        pltpu.VMEM((2 * t, w), BF16),
                        pltpu.VMEM((2 * t, LANES), F32),
                        pltpu.VMEM((2 * t, 2 * w), F32)],
        compiler_params=_params(("arbitrary", "arbitrary", "arbitrary")),
        name="moba_attn_prompt",
    )(rel_bias, z, z, z)


def _masked_query_rows(q, lane_starts, width):
    lane = lax.broadcasted_iota(jnp.int32, q.shape, 1)
    return jnp.concatenate(
        [jnp.where(jnp.logical_and(lane >= s, lane < s + width), q, 0.0) for s in lane_starts], axis=0)


def _group_bias(dist, rel_ref, cols):
    return jnp.concatenate([_rel_bias(dist, rel_ref, col) for col in cols], axis=0)


def _pad_to_page(rows):
    pad = jnp.zeros((PAGE_SIZE - rows.shape[0], rows.shape[1]), rows.dtype)
    return jnp.concatenate([rows, pad], axis=0).astype(BF16)


def _diff_decode_kernel(pt_ref, rel_ref, lamv_ref, g_ref, q_ref, kn_ref, vn_ref, *rest, lam_init):
    del pt_ref
    npg = PAGES_PER_STEP
    kp_refs, vp_refs = rest[:npg], rest[npg:2 * npg]
    o_ref, wq_ref, m_ref, l_ref, acc_ref = rest[2 * npg:]
    c = pl.program_id(1)
    last = pl.num_programs(1) - 1
    r_new = q_ref.shape[1]
    w = 2 * HD_A
    grp = 2 * r_new
    rows = H_A * grp
    flat = PAGE_SIZE * H_A
    cols = [mp * H_A + h for h in range(H_A) for mp in range(2)]

    @pl.when(c == 0)
    def _():
        q = q_ref[0] * (HD_A ** -0.5)
        lane = lax.broadcasted_iota(jnp.int32, (r_new, w), 1)
        parts = []
        for h in range(H_A):
            qh = q[:, h * w:(h + 1) * w]
            parts += [jnp.where(lane < HD_A, qh, 0.0), jnp.where(lane >= HD_A, qh, 0.0)]
        wq_ref[...] = jnp.concatenate(parts, axis=0).astype(BF16)
        m_ref[...] = jnp.full(m_ref.shape, NEG, F32)
        l_ref[...] = jnp.zeros(l_ref.shape, F32)
        acc_ref[...] = jnp.zeros(acc_ref.shape, F32)

    q_head = lax.broadcasted_iota(jnp.int32, (rows, flat), 0) // grp
    k_head = lax.broadcasted_iota(jnp.int32, (rows, flat), 1) % H_A
    head_ok = q_head == k_head

    s_pages = []
    for p in range(npg):
        kf = kp_refs[p][...].reshape(flat, w).astype(BF16)
        s = _dot_nt(wq_ref[...], kf)
        if p == npg - 1:
            r = lax.broadcasted_iota(jnp.int32, (r_new, flat), 0)
            j = lax.broadcasted_iota(jnp.int32, (r_new, flat), 1) // H_A
            dist = jnp.where(c == last, r + PAGE_SIZE - j, FAR_DISTANCE)
            s = s + _group_bias(dist, rel_ref, cols)
        s_pages.append(jnp.where(head_ok, s, NEG))
    m_prev = m_ref[...]
    m_new = m_prev
    for s in s_pages:
        m_new = jnp.maximum(m_new, jnp.max(s, axis=-1, keepdims=True))
    alpha = jnp.exp(m_prev - m_new)
    l_new = alpha * l_ref[...]
    acc = alpha * acc_ref[...]
    for p, s in enumerate(s_pages):
        pexp = jnp.exp(s - m_new)
        l_new = l_new + jnp.sum(pexp, axis=-1, keepdims=True)
        vf = vp_refs[p][...].reshape(flat, w).astype(BF16)
        acc = acc + jnp.dot(pexp.astype(BF16), vf, preferred_element_type=F32)
    m_ref[...] = m_new
    l_ref[...] = l_new
    acc_ref[...] = acc

    @pl.when(c == last)
    def _():
        kn = jnp.concatenate([kn_ref[0, :, h * w:(h + 1) * w] for h in range(H_A)], axis=0)
        vn = jnp.concatenate([vn_ref[0, :, h * w:(h + 1) * w] for h in range(H_A)], axis=0)
        r = lax.broadcasted_iota(jnp.int32, (r_new, PAGE_SIZE), 0)
        j = lax.broadcasted_iota(jnp.int32, (r_new, PAGE_SIZE), 1) % r_new
        s = _dot_nt(wq_ref[...], _pad_to_page(kn)) + _group_bias(r - j, rel_ref, cols)
        qr = lax.broadcasted_iota(jnp.int32, s.shape, 0)
        kc = lax.broadcasted_iota(jnp.int32, s.shape, 1)
        ok = jnp.logical_and(kc // r_new == qr // grp, kc % r_new <= qr % r_new)
        _online_softmax_step(jnp.where(ok, s, NEG), _pad_to_page(vn), m_ref, l_ref, acc_ref)

        o = acc_ref[...] / l_ref[...]
        lv = lamv_ref[...]
        lam = (jnp.exp(jnp.sum(lv[0:1] * lv[1:2], axis=-1, keepdims=True))
               - jnp.exp(jnp.sum(lv[2:3] * lv[3:4], axis=-1, keepdims=True)) + lam_init)
        for h in range(H_A):
            o1 = o[h * grp:h * grp + r_new]
            o2 = o[h * grp + r_new:(h + 1) * grp]
            o_ref[0, :, h * w:(h + 1) * w] = _rms(o1 - lam * o2, g_ref[...]) * (1.0 - lam_init)


def diff_attention_decode(z, cache_k, cache_v, page_table, layer, rel_bias, lamv, subln_g, lam_init):
    n_seq, r_new, _ = z.shape
    n_pages = page_table.shape[1]
    assert n_pages % PAGES_PER_STEP == 0 and r_new == SUBLANES and H_A * r_new <= PAGE_SIZE
    w = 2 * HD_A
    rows = 2 * H_A * r_new
    seq_block = lambda col: pl.BlockSpec((1, r_new, W_A), lambda b, c, pt: (b, 0, col // W_A))
    page_specs = [pl.BlockSpec((None, PAGE_SIZE, None, H_A, w),
                               lambda b, c, pt, p=p: (pt[b, c * PAGES_PER_STEP + p], 0, layer, 0, 0))
                  for p in range(PAGES_PER_STEP)]
    grid_spec = pltpu.PrefetchScalarGridSpec(
        num_scalar_prefetch=1,
        grid=(n_seq, n_pages // PAGES_PER_STEP),
        in_specs=[pl.BlockSpec(memory_space=pltpu.SMEM),
                  pl.BlockSpec((4, HD_A), lambda b, c, pt: (0, 0)),
                  pl.BlockSpec((1, w), lambda b, c, pt: (0, 0)),
                  seq_block(COL_QA), seq_block(COL_KA), seq_block(COL_VA)] + page_specs + page_specs,
        out_specs=pl.BlockSpec((1, r_new, W_A), lambda b, c, pt: (b, 0, 0)),
        scratch_shapes=[pltpu.VMEM((rows, w), BF16),
                        pltpu.VMEM((rows, 1), F32),
                        pltpu.VMEM((rows, 1), F32),
                        pltpu.VMEM((rows, w), F32)])
    return pl.pallas_call(
        functools.partial(_diff_decode_kernel, lam_init=lam_init),
        grid_spec=grid_spec,
        out_shape=jax.ShapeDtypeStruct((n_seq, r_new, W_A), F32),
        compiler_params=_params(("arbitrary", "arbitrary")),
        name="diff_attn_decode",
    )(page_table, rel_bias, lamv, subln_g.reshape(1, w), z, z, z,
      *([cache_k] * PAGES_PER_STEP), *([cache_v] * PAGES_PER_STEP))


def _new_rows_scores(wq, k_new, rel_ref, cols):
    r_new = k_new.shape[0]
    r = lax.broadcasted_iota(jnp.int32, (r_new, PAGE_SIZE), 0)
    j = lax.broadcasted_iota(jnp.int32, (r_new, PAGE_SIZE), 1)
    s = _dot_nt(wq, _pad_to_page(k_new)) + _group_bias(r - j, rel_ref, cols)
    q_row = lax.broadcasted_iota(jnp.int32, s.shape, 0) % r_new
    return jnp.where(lax.broadcasted_iota(jnp.int32, s.shape, 1) <= q_row, s, NEG)


def _moba_decode_kernel(pt_ref, rel_ref, q_ref, kn_ref, vn_ref, *rest):
    del pt_ref
    npg = PAGES_PER_STEP
    kp_refs, vp_refs = rest[:npg], rest[npg:2 * npg]
    o_ref, wq_ref, wqf_ref, kmean_ref, mblk_ref, lblk_ref, accblk_ref = rest[2 * npg:]
    c = pl.program_id(1)
    last = pl.num_programs(1) - 1
    r_new = q_ref.shape[1]
    nb = accblk_ref.shape[0]
    pages_per_block = MOBA_BLOCK // PAGE_SIZE
    blocks_per_step = npg // pages_per_block
    cols = [2 * H_A + h for h in range(H_B)]

    @pl.when(c == 0)
    def _():
        qf = _masked_query_rows(q_ref[0], [h * HD_B for h in range(H_B)], HD_B)
        wqf_ref[...] = qf
        wq_ref[...] = (qf * (HD_B ** -0.5)).astype(BF16)
        kmean_ref[...] = jnp.zeros(kmean_ref.shape, F32)
        mblk_ref[...] = jnp.full(mblk_ref.shape, NEG, F32)
        lblk_ref[...] = jnp.zeros(lblk_ref.shape, F32)

    blk_iota = lax.broadcasted_iota(jnp.int32, mblk_ref.shape, 1)
    mean_lane = lax.broadcasted_iota(jnp.int32, kmean_ref.shape, 1)
    for jb in range(blocks_per_step):
        n = c * blocks_per_step + jb
        pages = range(jb * pages_per_block, (jb + 1) * pages_per_block)
        kts = [kp_refs[p][...].reshape(W_B, PAGE_SIZE) for p in pages]
        s_parts = [jnp.dot(wq_ref[...], kt.astype(BF16), preferred_element_type=F32) for kt in kts]
        if jb == blocks_per_step - 1:
            r = lax.broadcasted_iota(jnp.int32, (r_new, PAGE_SIZE), 0)
            j = lax.broadcasted_iota(jnp.int32, (r_new, PAGE_SIZE), 1)
            dist = jnp.where(c == last, r + PAGE_SIZE - j, FAR_DISTANCE)
            s_parts[-1] = s_parts[-1] + _group_bias(dist, rel_ref, cols)
        s = jnp.concatenate(s_parts, axis=-1)
        m = jnp.max(s, axis=-1, keepdims=True)
        pexp = jnp.exp(s - m)
        acc = sum(_dot_nt(pexp[:, i * PAGE_SIZE:(i + 1) * PAGE_SIZE].astype(BF16),
                          vp_refs[p][...].reshape(W_B, PAGE_SIZE).astype(BF16))
                  for i, p in enumerate(pages))
        accblk_ref[n] = acc
        mblk_ref[...] = jnp.where(blk_iota == n, m, mblk_ref[...])
        lblk_ref[...] = jnp.where(blk_iota == n, jnp.sum(pexp, axis=-1, keepdims=True), lblk_ref[...])
        kmean = sum(jnp.sum(kt, axis=-1, keepdims=True) for kt in kts) * (1.0 / MOBA_BLOCK)
        kmean_ref[...] = jnp.where(mean_lane == n, kmean, kmean_ref[...])

    @pl.when(c == last)
    def _():
        gate = jnp.dot(wqf_ref[...], kmean_ref[...], precision=lax.Precision.HIGHEST,
                       preferred_element_type=F32)
        sel = _top_k_mask(gate, blk_iota < nb, MOBA_TOPK)
        s_own = _new_rows_scores(wq_ref[...], kn_ref[0], rel_ref, cols)
        mblk = jnp.where(sel, mblk_ref[...], NEG)
        m_all = jnp.maximum(jnp.max(mblk, axis=-1, keepdims=True), jnp.max(s_own, axis=-1, keepdims=True))
        wgt = jnp.where(sel, jnp.exp(mblk - m_all), 0.0)
        p_own = jnp.exp(s_own - m_all)
        denom = (jnp.sum(wgt * lblk_ref[...], axis=-1, keepdims=True)
                 + jnp.sum(p_own, axis=-1, keepdims=True))
        acc = jnp.dot(p_own.astype(BF16), _pad_to_page(vn_ref[0]), preferred_element_type=F32)
        for n in range(nb):
            acc = acc + wgt[:, n:n + 1] * accblk_ref[n]
        o = acc / denom
        lane = lax.broadcasted_iota(jnp.int32, (r_new, W_B), 1)
        out = jnp.zeros((r_new, W_B), F32)
        for h in range(H_B):
            in_head = jnp.logical_and(lane >= h * HD_B, lane < (h + 1) * HD_B)
            out = out + jnp.where(in_head, o[h * r_new:(h + 1) * r_new, :], 0.0)
        o_ref[0] = out


def moba_attention_decode(z, cache_kt, cache_vt, page_table, layer, rel_bias):
    n_seq, r_new, _ = z.shape
    n_pages = page_table.shape[1]
    past = n_pages * PAGE_SIZE
    assert n_pages % PAGES_PER_STEP == 0 and past % MOBA_BLOCK == 0 and r_new == SUBLANES
    assert PAGES_PER_STEP % (MOBA_BLOCK // PAGE_SIZE) == 0 and r_new <= MOBA_BLOCK
    nb = past // MOBA_BLOCK
    assert nb <= LANES
    rows = H_B * r_new
    seq_block = lambda col: pl.BlockSpec((1, r_new, W_B), lambda b, c, pt: (b, 0, col // W_B))
    page_specs = [pl.BlockSpec((None, None, H_B, HD_B, PAGE_SIZE),
                               lambda b, c, pt, p=p: (pt[b, c * PAGES_PER_STEP + p], layer, 0, 0, 0))
                  for p in range(PAGES_PER_STEP)]
    grid_spec = pltpu.PrefetchScalarGridSpec(
        num_scalar_prefetch=1,
        grid=(n_seq, n_pages // PAGES_PER_STEP),
        in_specs=[pl.BlockSpec(memory_space=pltpu.SMEM),
                  seq_block(COL_QB), seq_block(COL_KB), seq_block(COL_VB)] + page_specs + page_specs,
        out_specs=pl.BlockSpec((1, r_new, W_B), lambda b, c, pt: (b, 0, 0)),
        scratch_shapes=[pltpu.VMEM((rows, W_B), BF16),
                        pltpu.VMEM((rows, W_B), F32),
                        pltpu.VMEM((W_B, LANES), F32),
                        pltpu.VMEM((rows, LANES), F32),
                        pltpu.VMEM((rows, LANES), F32),
                        pltpu.VMEM((nb, rows, W_B), F32)])
    return pl.pallas_call(
        _moba_decode_kernel,
        grid_spec=grid_spec,
        out_shape=jax.ShapeDtypeStruct((n_seq, r_new, W_B), F32),
        compiler_params=_params(("arbitrary", "arbitrary")),
        name="moba_attn_decode",
    )(page_table, rel_bias, z, z, z, *([cache_kt] * PAGES_PER_STEP), *([cache_vt] * PAGES_PER_STEP))


def _sigmoid(x):
    return 1.0 / (1.0 + jnp.exp(-x))


def _mix_out_kernel(oa_ref, ob_ref, ga_ref, gb_ref, x_ref, wa_ref, wb_ref, wo_ref, o_ref):
    ya = jnp.dot(oa_ref[...].astype(BF16), wa_ref[...], preferred_element_type=F32)
    yb = jnp.dot(ob_ref[...].astype(BF16), wb_ref[...], preferred_element_type=F32)
    mixed = _sigmoid(ga_ref[...]) * ya + _sigmoid(gb_ref[...]) * yb
    o_ref[...] = x_ref[...] + jnp.dot(mixed.astype(BF16), wo_ref[...], preferred_element_type=F32)


def mix_out(oa, ob, z, x, wa, wb, wo):
    n, d = x.shape
    tm = min(MIX_ROW_TILE, n)
    assert n % tm == 0
    full = lambda a: pl.BlockSpec(a.shape, lambda i: (0, 0))
    return pl.pallas_call(
        _mix_out_kernel,
        grid=(n // tm,),
        in_specs=[pl.BlockSpec((tm, W_A), lambda i: (i, 0)),
                  pl.BlockSpec((tm, W_B), lambda i: (i, 0)),
                  pl.BlockSpec((tm, d), lambda i: (i, COL_GA // d)),
                  pl.BlockSpec((tm, d), lambda i: (i, COL_GB // d)),
                  pl.BlockSpec((tm, d), lambda i: (i, 0)),
                  full(wa), full(wb), full(wo)],
        out_specs=pl.BlockSpec((tm, d), lambda i: (i, 0)),
        out_shape=jax.ShapeDtypeStruct((n, d), F32),
        compiler_params=_params(("arbitrary",)),
        name="mix_out",
    )(oa, ob, z, z, x, wa, wb, wo)


def _mem_attn_kernel(x_ref, g_ref, k_ref, v_ref, wq_ref, wo_ref, o_ref):
    x = x_ref[0]
    q = jnp.dot(_rms(x, g_ref[...]).astype(BF16), wq_ref[...], preferred_element_type=F32)
    heads = []
    for h in range(H_M):
        sl = slice(h * HD_M, (h + 1) * HD_M)
        s = _dot_nt((q[:, sl] * (HD_M ** -0.5)).astype(BF16), k_ref[0, :, sl].astype(BF16))
        p = jnp.exp(s - jnp.max(s, axis=-1, keepdims=True))
        oh = jnp.dot(p.astype(BF16), v_ref[0, :, sl].astype(BF16), preferred_element_type=F32)
        heads.append(oh / jnp.sum(p, axis=-1, keepdims=True))
    o = jnp.concatenate(heads, axis=-1)
    o_ref[0] = x + jnp.dot(o.astype(BF16), wo_ref[...], preferred_element_type=F32)


def memory_attention(x, g, mem_k, k_col, mem_v, v_col, wq, wo):
    grp, t_len, d = x.shape
    n_mem = mem_k.shape[1]
    tm = min(MIX_ROW_TILE, t_len)
    assert t_len % tm == 0
    return pl.pallas_call(
        _mem_attn_kernel,
        grid=(grp, t_len // tm),
        in_specs=[pl.BlockSpec((1, tm, d), lambda b, i: (b, i, 0)),
                  pl.BlockSpec((1, d), lambda b, i: (0, 0)),
                  pl.BlockSpec((1, n_mem, W_M), lambda b, i: (b, 0, k_col)),
                  pl.BlockSpec((1, n_mem, W_M), lambda b, i: (b, 0, v_col)),
                  pl.BlockSpec(wq.shape, lambda b, i: (0, 0)),
                  pl.BlockSpec(wo.shape, lambda b, i: (0, 0))],
        out_specs=pl.BlockSpec((1, tm, d), lambda b, i: (b, i, 0)),
        out_shape=jax.ShapeDtypeStruct((grp, t_len, d), F32),
        compiler_params=_params(("arbitrary", "arbitrary")),
        name="memory_attention",
    )(x, g.reshape(1, d), mem_k, mem_v, wq, wo)


def _swiglu_hidden(xn, wg, wu):
    a = jnp.dot(xn, wg, preferred_element_type=F32)
    return a * _sigmoid(a) * jnp.dot(xn, wu, preferred_element_type=F32)


def _dense_ffn_kernel(x_ref, g_ref, wg_ref, wu_ref, wd_ref, o_ref, xn_ref, acc_ref):
    j = pl.program_id(1)

    @pl.when(j == 0)
    def _():
        xn_ref[...] = _rms(x_ref[...], g_ref[...]).astype(BF16)
        acc_ref[...] = jnp.zeros(acc_ref.shape, F32)

    hid = _swiglu_hidden(xn_ref[...], wg_ref[...], wu_ref[...])
    acc_ref[...] += jnp.dot(hid.astype(BF16), wd_ref[...], preferred_element_type=F32)

    @pl.when(j == pl.num_programs(1) - 1)
    def _():
        o_ref[...] = x_ref[...] + acc_ref[...]


def dense_ffn(x, g, wg, wu, wd):
    n, d = x.shape
    f = wg.shape[1]
    tm = min(ROW_TILE, n)
    tf = FF_TILE
    assert n % tm == 0 and f % tf == 0
    return pl.pallas_call(
        _dense_ffn_kernel,
        grid=(n // tm, f // tf),
        in_specs=[pl.BlockSpec((tm, d), lambda i, j: (i, 0)),
                  pl.BlockSpec((1, d), lambda i, j: (0, 0)),
                  pl.BlockSpec((d, tf), lambda i, j: (0, j)),
                  pl.BlockSpec((d, tf), lambda i, j: (0, j)),
                  pl.BlockSpec((tf, d), lambda i, j: (j, 0))],
        out_specs=pl.BlockSpec((tm, d), lambda i, j: (i, 0)),
        out_shape=jax.ShapeDtypeStruct((n, d), F32),
        scratch_shapes=[pltpu.VMEM((tm, d), BF16), pltpu.VMEM((tm, d), F32)],
        compiler_params=_params(("arbitrary", "arbitrary")),
        name="dense_ffn",
    )(x, g.reshape(1, d), wg, wu, wd)


def _moe_ffn_kernel(x_ref, g_ref, r_ref, wg_ref, wu_ref, wd_ref, o_ref,
                    xn_ref, gate_ref, rank_ref, rankt_ref, membt_ref, xc_ref, yc_ref, acc_ref, cnt_ref, *, cap):
    e = pl.program_id(1)
    j = pl.program_id(2)
    last_j = pl.num_programs(2) - 1
    tm = x_ref.shape[0]

    @pl.when(jnp.logical_and(e == 0, j == 0))
    def _():
        xn = _rms(x_ref[...], g_ref[...])
        xn_ref[...] = xn.astype(BF16)
        acc_ref[...] = jnp.zeros(acc_ref.shape, F32)
        logits = jnp.dot(xn, r_ref[...], precision=lax.Precision.HIGHEST, preferred_element_type=F32)
        lane = lax.broadcasted_iota(jnp.int32, logits.shape, 1).astype(F32)
        logits = jnp.where(lane < N_EXPERTS, logits, -jnp.inf)
        v1 = jnp.max(logits, axis=-1, keepdims=True)
        i1 = jnp.min(jnp.where(logits == v1, lane, float(LANES)), axis=-1, keepdims=True)
        rest = jnp.where(lane == i1, -jnp.inf, logits)
        v2 = jnp.max(rest, axis=-1, keepdims=True)
        i2 = jnp.min(jnp.where(rest == v2, lane, float(LANES)), axis=-1, keepdims=True)
        e2 = jnp.exp(v2 - v1)
        denom = 1.0 + e2
        gate_ref[...] = jnp.where(lane == i1, 1.0 / denom, 0.0) + jnp.where(lane == i2, e2 / denom, 0.0)
        memb = jnp.where(gate_ref[...] > 0.0, 1.0, 0.0)
        r = lax.broadcasted_iota(jnp.int32, (tm, tm), 0)
        c = lax.broadcasted_iota(jnp.int32, (tm, tm), 1)
        rank_ref[...] = jnp.dot(jnp.where(c < r, 1.0, 0.0).astype(BF16), memb.astype(BF16),
                                preferred_element_type=F32)
        membt = memb.T
        membt_ref[...] = membt
        rankt_ref[...] = jnp.dot(membt.astype(BF16), jnp.where(r < c, 1.0, 0.0).astype(BF16),
                                 preferred_element_type=F32)

    def expert_column(ref):
        lane = lax.broadcasted_iota(jnp.int32, ref.shape, 1)
        return jnp.sum(jnp.where(lane == e, ref[...], 0.0), axis=-1, keepdims=True)

    @pl.when(j == 0)
    def _():
        cnt_ref[0] = jnp.sum(jnp.where(expert_column(gate_ref) > 0.0, 1.0, 0.0)).astype(jnp.int32)

    n_chunks = (cnt_ref[0] + (cap - 1)) // cap

    @pl.when(j == 0)
    def _():
        rt = rankt_ref[pl.ds(e, 1), :]
        mt = membt_ref[pl.ds(e, 1), :]

        def gather(ci, carry):
            base = pl.multiple_of(ci * cap, cap)
            slot = (lax.broadcasted_iota(jnp.int32, (cap, tm), 0) + base).astype(F32)
            onehot = jnp.where(jnp.logical_and(slot == rt, mt > 0.5), 1.0, 0.0).astype(BF16)
            xc_ref[pl.ds(base, cap), :] = jnp.dot(onehot, xn_ref[...], preferred_element_type=F32).astype(BF16)
            yc_ref[pl.ds(base, cap), :] = jnp.zeros((cap, yc_ref.shape[1]), F32)
            return carry

        lax.fori_loop(0, n_chunks, gather, 0)

    def ffn(ci, carry):
        rows = pl.ds(pl.multiple_of(ci * cap, cap), cap)
        hid = _swiglu_hidden(xc_ref[rows, :], wg_ref[0], wu_ref[0])
        yc_ref[rows, :] += jnp.dot(hid.astype(BF16), wd_ref[0], preferred_element_type=F32)
        return carry

    lax.fori_loop(0, n_chunks, ffn, 0)

    @pl.when(j == last_j)
    def _():
        rank_e = expert_column(rank_ref)
        gate_e = expert_column(gate_ref)

        def scatter(ci, carry):
            base = pl.multiple_of(ci * cap, cap)
            slot = (lax.broadcasted_iota(jnp.int32, (tm, cap), 1) + base).astype(F32)
            onehot_t = jnp.where(slot == rank_e, gate_e, 0.0).astype(BF16)
            acc_ref[...] += jnp.dot(onehot_t, yc_ref[pl.ds(base, cap), :].astype(BF16), preferred_element_type=F32)
            return carry

        lax.fori_loop(0, n_chunks, scatter, 0)

    @pl.when(jnp.logical_and(e == pl.num_programs(1) - 1, j == last_j))
    def _():
        o_ref[...] = x_ref[...] + acc_ref[...]


def moe_ffn(x, g, router_padded, wg, wu, wd):
    n, d = x.shape
    n_exp, _, f = wg.shape
    tm = min(ROW_TILE, n)
    tf = FF_TILE
    cap = min(MOE_CHUNK, tm)
    cap_rows = -(-tm // cap) * cap
    assert n % tm == 0 and f % tf == 0 and n_exp == N_EXPERTS and tm % LANES == 0 and cap % (2 * SUBLANES) == 0
    return pl.pallas_call(
        functools.partial(_moe_ffn_kernel, cap=cap),
        grid=(n // tm, n_exp, f // tf),
        in_specs=[pl.BlockSpec((tm, d), lambda i, e, j: (i, 0)),
                  pl.BlockSpec((1, d), lambda i, e, j: (0, 0)),
                  pl.BlockSpec((d, LANES), lambda i, e, j: (0, 0)),
                  pl.BlockSpec((1, d, tf), lambda i, e, j: (e, 0, j)),
                  pl.BlockSpec((1, d, tf), lambda i, e, j: (e, 0, j)),
                  pl.BlockSpec((1, tf, d), lambda i, e, j: (e, j, 0))],
        out_specs=pl.BlockSpec((tm, d), lambda i, e, j: (i, 0)),
        out_shape=jax.ShapeDtypeStruct((n, d), F32),
        scratch_shapes=[pltpu.VMEM((tm, d), BF16),
                        pltpu.VMEM((tm, LANES), F32),
                        pltpu.VMEM((tm, LANES), F32),
                        pltpu.VMEM((LANES, tm), F32),
                        pltpu.VMEM((LANES, tm), F32),
                        pltpu.VMEM((cap_rows, d), BF16),
                        pltpu.VMEM((cap_rows, d), F32),
                        pltpu.VMEM((tm, d), F32),
                        pltpu.SMEM((1,), jnp.int32)],
        compiler_params=_params(("arbitrary", "arbitrary", "arbitrary")),
        name="moe_ffn",
    )(x, g.reshape(1, d), router_padded, wg, wu, wd)


def kernel(x_prompt, x_sample, mem_prompt, cache_a_k, cache_a_v, cache_b_k, cache_b_v, cache_mem_k, cache_mem_v, page_table, rel_bias, norm_mix, w_in, lambda_q1, lambda_k1, lambda_q2, lambda_k2, subln_gain, w_br_a, w_br_b, w_out, norm_mem_q, norm_mem_kv, w_mem_q, w_mem_kv, w_mem_o, norm_ffn, ffn_w_gate, ffn_w_up, ffn_w_down, moe_router, moe_w_gate, moe_w_up, moe_w_down, norm_final):
    bsz, seq, d = x_prompt.shape
    n_seq, r_new, _ = x_sample.shape
    n_mem = mem_prompt.shape[1]
    depth = w_in.shape[0]

    cb_kt = jnp.transpose(cache_b_k, (0, 2, 3, 4, 1))
    cb_vt = jnp.transpose(cache_b_v, (0, 2, 3, 4, 1))
    cm_k = cache_mem_k.reshape(n_seq, n_mem, depth * W_M)
    cm_v = cache_mem_v.reshape(n_seq, n_mem, depth * W_M)

    hp = x_prompt.reshape(bsz * seq, d)
    hs = x_sample.reshape(n_seq * r_new, d)
    zs_p, zs_s, mkvs = [], [], []
    for l in range(depth):
        lam_init = 0.8 - 0.6 * math.exp(-0.3 * l)
        lamv = jnp.stack([lambda_q1[l], lambda_k1[l], lambda_q2[l], lambda_k2[l]]).astype(F32)
        w_in_l = w_in[l].astype(BF16)
        wa, wb, wo = w_br_a[l].astype(BF16), w_br_b[l].astype(BF16), w_out[l].astype(BF16)

        zp = norm_matmul(hp, norm_mix[l], w_in_l)
        zp3 = zp.reshape(bsz, seq, IN_COLS)
        oa = diff_attention_prompt(zp3, rel_bias, lamv, subln_gain[l], lam_init)
        ob = moba_attention_prompt(zp3, rel_bias)
        hp = mix_out(oa.reshape(bsz * seq, W_A), ob.reshape(bsz * seq, W_B), zp, hp, wa, wb, wo)
        zs_p.append(zp3)

        zs = norm_matmul(hs, norm_mix[l], w_in_l)
        zs3 = zs.reshape(n_seq, r_new, IN_COLS)
        oa_s = diff_attention_decode(zs3, cache_a_k, cache_a_v, page_table, l, rel_bias, lamv, subln_gain[l], lam_init)
        ob_s = moba_attention_decode(zs3, cb_kt, cb_vt, page_table, l, rel_bias)
        hs = mix_out(oa_s.reshape(n_seq * r_new, W_A), ob_s.reshape(n_seq * r_new, W_B), zs, hs, wa, wb, wo)
        zs_s.append(zs3)

        mkv = norm_matmul(mem_prompt.reshape(bsz * n_mem, d), norm_mem_kv[l], w_mem_kv[l].astype(BF16))
        mkv3 = mkv.reshape(bsz, n_mem, 2 * W_M)
        mkvs.append(mkv3)
        wq, wmo = w_mem_q[l].astype(BF16), w_mem_o[l].astype(BF16)
        hp = memory_attention(hp.reshape(bsz, seq, d), norm_mem_q[l], mkv3, 0, mkv3, 1, wq, wmo).reshape(bsz * seq, d)
        hs = memory_attention(hs.reshape(n_seq, r_new, d), norm_mem_q[l], cm_k, l, cm_v, l, wq, wmo).reshape(n_seq * r_new, d)

        i = l // 2
        if l % 2 == 0:
            wg, wu, wd = ffn_w_gate[i].astype(BF16), ffn_w_up[i].astype(BF16), ffn_w_down[i].astype(BF16)
            hp = dense_ffn(hp, norm_ffn[l], wg, wu, wd)
            hs = dense_ffn(hs, norm_ffn[l], wg, wu, wd)
        else:
            wg, wu, wd = moe_w_gate[i].astype(BF16), moe_w_up[i].astype(BF16), moe_w_down[i].astype(BF16)
            router = jnp.pad(moe_router[i].astype(F32), ((0, 0), (0, LANES - N_EXPERTS)))
            hp = moe_ffn(hp, norm_ffn[l], router, wg, wu, wd)
            hs = moe_ffn(hs, norm_ffn[l], router, wg, wu, wd)

    y_prompt = rmsnorm(hp, norm_final).reshape(bsz, seq, d)
    y_sample = rmsnorm(hs, norm_final).reshape(n_seq, r_new, d)

    def stacked(zs, col, width, heads):
        rows = jnp.stack([z[:, :, col:col + width] for z in zs], axis=2)
        return rows.reshape(rows.shape[0], rows.shape[1], depth, heads, width // heads)

    new_mem_k = jnp.stack([m[:, :, :W_M] for m in mkvs], axis=2).reshape(bsz, n_mem, depth, H_M, HD_M)
    new_mem_v = jnp.stack([m[:, :, W_M:] for m in mkvs], axis=2).reshape(bsz, n_mem, depth, H_M, HD_M)
    return (y_prompt, y_sample,
            stacked(zs_p, COL_KA, W_A, H_A), stacked(zs_p, COL_VA, W_A, H_A),
            stacked(zs_p, COL_KB, W_B, H_B), stacked(zs_p, COL_VB, W_B, H_B),
            new_mem_k, new_mem_v,
            stacked(zs_s, COL_KA, W_A, H_A), stacked(zs_s, COL_VA, W_A, H_A),
            stacked(zs_s, COL_KB, W_B, H_B), stacked(zs_s, COL_VB, W_B, H_B))
```

```python
import functools
import math

import numpy as np
import jax
import jax.numpy as jnp
from jax import lax
from jax.experimental import pallas as pl
from jax.experimental.pallas import tpu as pltpu

F32 = jnp.float32
BF16 = jnp.bfloat16

D_MODEL = 1024
DEPTH = 2
PAGE_SIZE = 128
H_A = 4
HD_A = 64
W_A = H_A * 2 * HD_A
H_B = 8
HD_B = 64
W_B = H_B * HD_B
MOBA_BLOCK = 256
MOBA_TOPK = 3
H_M = 4
HD_M = 128
W_M = H_M * HD_M
N_BUCKETS = 32
MAX_DISTANCE = 128
N_EXPERTS = 8
TOP_K_EXPERTS = 2
IN_COLS = 3 * W_A + 3 * W_B + 2 * D_MODEL
EPS = 1e-6
NEG = -1e30
LOG2E = math.log2(math.e)

LANES = 128
SUBLANES = 8
VMEM_LIMIT_BYTES = 56 * 1024 * 1024

ATTN_TILE = 2 * MOBA_BLOCK
ATTN_ROW_CHUNK = MOBA_BLOCK
ATTN_UNROLL = 4
PAGES_PER_STEP = 8
ROW_TILE = 1024
COL_TILE = 1024
FF_TILE = 256
MOE_CHUNK = 320
MIX_ROW_TILE = 512

COL_QA, COL_KA, COL_VA = 0, W_A, 2 * W_A
COL_QB, COL_KB, COL_VB = 3 * W_A, 3 * W_A + W_B, 3 * W_A + 2 * W_B
COL_GA = 3 * W_A + 3 * W_B
COL_GB = COL_GA + D_MODEL


def _t5_bucket_starts():
    n = np.arange(0, 4 * MAX_DISTANCE, dtype=np.int64)
    max_exact = N_BUCKETS // 2
    nf = np.maximum(n, 1).astype(np.float32)
    large = max_exact + (np.log(nf / np.float32(max_exact))
                         / np.float32(math.log(MAX_DISTANCE / max_exact))
                         * np.float32(N_BUCKETS - max_exact)).astype(np.int32)
    bucket = np.where(n < max_exact, n, np.minimum(large, N_BUCKETS - 1))
    assert np.all(np.diff(bucket) >= 0) and bucket[-1] == N_BUCKETS - 1
    starts = [int(np.argmax(bucket >= b)) for b in range(N_BUCKETS)]
    assert starts[-1] <= MAX_DISTANCE
    return starts


BUCKET_START = _t5_bucket_starts()
FAR_DISTANCE = BUCKET_START[-1]


def _params(semantics):
    return pltpu.CompilerParams(dimension_semantics=semantics, vmem_limit_bytes=VMEM_LIMIT_BYTES)


def _rms(x, g):
    return x * lax.rsqrt(jnp.mean(x * x, axis=-1, keepdims=True) + EPS) * g


def _rel_bias(dist, rel_ref, col, scale=1.0):
    last = rel_ref[N_BUCKETS - 1, col]
    out = jnp.zeros(dist.shape, F32)
    for b in range(N_BUCKETS - 2, -1, -1):
        out = jnp.where(dist < BUCKET_START[b + 1], (rel_ref[b, col] - last) * scale, out)
    return out


def _dot_nt(a, b, precision=None):
    return lax.dot_general(a, b, (((1,), (1,)), ((), ())), precision=precision,
                           preferred_element_type=F32)


def _online_softmax_step(s, v, m_ref, l_ref, acc_ref):
    m_prev = m_ref[...]
    m_new = jnp.maximum(m_prev, jnp.max(s, axis=-1, keepdims=True))
    alpha = jnp.exp(m_prev - m_new)
    p = jnp.exp(s - m_new)
    l_ref[...] = alpha * l_ref[...] + jnp.sum(p, axis=-1, keepdims=True)
    acc_ref[...] = alpha * acc_ref[...] + jnp.dot(p.astype(BF16), v, preferred_element_type=F32)
    m_ref[...] = m_new


def _top_k_mask(gate, valid, k):
    idx = lax.broadcasted_iota(jnp.int32, gate.shape, 1).astype(F32)
    g = jnp.where(valid, gate, NEG)
    sel = jnp.zeros(gate.shape, F32)
    for _ in range(k):
        mx = jnp.max(g, axis=-1, keepdims=True)
        first = jnp.min(jnp.where(g == mx, idx, float(gate.shape[-1])), axis=-1, keepdims=True)
        pick = idx == first
        sel = jnp.where(pick, 1.0, sel)
        g = jnp.where(pick, -jnp.inf, g)
    return jnp.logical_and(sel > 0.5, valid)


def _norm_matmul_kernel(x_ref, g_ref, w_ref, o_ref, xn_ref):
    @pl.when(pl.program_id(1) == 0)
    def _():
        xn_ref[...] = _rms(x_ref[...], g_ref[...]).astype(BF16)

    o_ref[...] = jnp.dot(xn_ref[...], w_ref[...], preferred_element_type=F32)


def norm_matmul(x, g, w_bf16):
    n, d = x.shape
    c = w_bf16.shape[1]
    tm = min(ROW_TILE, n)
    tn = min(COL_TILE, c)
    assert n % tm == 0 and c % tn == 0
    return pl.pallas_call(
        _norm_matmul_kernel,
        grid=(n // tm, c // tn),
        in_specs=[pl.BlockSpec((tm, d), lambda i, j: (i, 0)),
                  pl.BlockSpec((1, d), lambda i, j: (0, 0)),
                  pl.BlockSpec((d, tn), lambda i, j: (0, j))],
        out_specs=pl.BlockSpec((tm, tn), lambda i, j: (i, j)),
        out_shape=jax.ShapeDtypeStruct((n, c), F32),
        scratch_shapes=[pltpu.VMEM((tm, d), BF16)],
        compiler_params=_params(("arbitrary", "arbitrary")),
        name="norm_matmul",
    )(x, g.reshape(1, d), w_bf16)


def _rmsnorm_kernel(x_ref, g_ref, o_ref):
    o_ref[...] = _rms(x_ref[...], g_ref[...])


def rmsnorm(x, g):
    n, d = x.shape
    tm = min(ROW_TILE, n)
    return pl.pallas_call(
        _rmsnorm_kernel,
        grid=(n // tm,),
        in_specs=[pl.BlockSpec((tm, d), lambda i: (i, 0)), pl.BlockSpec((1, d), lambda i: (0, 0))],
        out_specs=pl.BlockSpec((tm, d), lambda i: (i, 0)),
        out_shape=jax.ShapeDtypeStruct((n, d), F32),
        compiler_params=_params(("arbitrary",)),
        name="final_rmsnorm",
    )(x, g.reshape(1, d))


def _stack_streams(q, half):
    lane = lax.broadcasted_iota(jnp.int32, q.shape, 1)
    return jnp.concatenate([jnp.where(lane < half, q, 0.0), jnp.where(lane >= half, q, 0.0)], axis=0)


def _fill_tile_bias(bias_ref, rel_ref, cols, t):
    r = lax.broadcasted_iota(jnp.int32, (t, t), 0)
    c = lax.broadcasted_iota(jnp.int32, (t, t), 1)
    d_diag = r - c
    for s, col in enumerate(cols):
        bias_ref[0, s * t:(s + 1) * t, :] = _rel_bias(d_diag + t, rel_ref, col, LOG2E)
        bias_ref[1, s * t:(s + 1) * t, :] = jnp.where(d_diag >= 0, _rel_bias(d_diag, rel_ref, col, LOG2E), NEG)


def _prepare_kv(k_ref, v_ref, kb_ref, va_ref, t):
    n_blocks = k_ref.shape[1] // t
    w = v_ref.shape[2]

    def body(n, carry):
        rows = pl.ds(pl.multiple_of(n * t, t), t)
        kb_ref[rows, :] = k_ref[0, rows, :].astype(BF16)
        va_ref[rows, :w] = v_ref[0, rows, :].astype(BF16)
        va_ref[rows, w:] = jnp.ones((t, w), BF16)
        return carry

    lax.fori_loop(0, n_blocks, body, 0)


def _flash_step(s, va, m_ref, acc_ref):
    m_prev = m_ref[...]
    m_new = jnp.maximum(m_prev, jnp.max(s, axis=-1, keepdims=True))
    alpha = jnp.exp2(m_prev - m_new)
    p = jnp.exp2(s - jnp.concatenate([m_new] * (s.shape[1] // LANES), axis=1))
    acc_ref[...] = (jnp.concatenate([alpha] * (acc_ref.shape[1] // LANES), axis=1) * acc_ref[...]
                    + jnp.dot(p.astype(BF16), va, preferred_element_type=F32))
    m_ref[...] = m_new


def _far_blocks(n_far, step):
    def group(g, carry):
        for u in range(ATTN_UNROLL):
            step(ATTN_UNROLL * g + u)
        return carry

    def single(kb, carry):
        step(kb)
        return carry

    lax.fori_loop(0, n_far // ATTN_UNROLL, group, 0)
    lax.fori_loop(n_far - n_far % ATTN_UNROLL, n_far, single, 0)


def _diff_attn_kernel(rel_ref, lamv_ref, g_ref, q_ref, k_ref, v_ref, o_ref,
                      bias_ref, kb_ref, va_ref, qs_ref, m_ref, acc_ref, *, lam_init):
    t = ATTN_TILE
    w = 2 * HD_A
    h = pl.program_id(1)
    qi = pl.program_id(2)

    @pl.when(qi == 0)
    def _():
        _fill_tile_bias(bias_ref, rel_ref, (h, H_A + h), t)
        _prepare_kv(k_ref, v_ref, kb_ref, va_ref, t)

    qs_ref[...] = _stack_streams(q_ref[0] * (HD_A ** -0.5 * LOG2E), HD_A).astype(BF16)
    m_ref[...] = jnp.full(m_ref.shape, NEG, F32)
    acc_ref[...] = jnp.zeros(acc_ref.shape, F32)

    def step(kb, bias):
        keys = pl.ds(pl.multiple_of(kb * t, t), t)
        for c in range(2 * t // ATTN_ROW_CHUNK):
            rows = slice(c * ATTN_ROW_CHUNK, (c + 1) * ATTN_ROW_CHUNK)
            s = _dot_nt(qs_ref[rows, :], kb_ref[keys, :])
            if bias is not None:
                s = s + bias[rows, :]
            _flash_step(s, va_ref[keys, :], m_ref.at[rows, :], acc_ref.at[rows, :])

    _far_blocks(jnp.maximum(qi - 1, 0), lambda kb: step(kb, None))

    @pl.when(qi >= 1)
    def _():
        step(qi - 1, bias_ref[0])

    step(qi, bias_ref[1])

    acc = acc_ref[...]
    o = acc[:, :w] / acc[:, w:]
    lv = lamv_ref[...]
    lam = (jnp.exp(jnp.sum(lv[0:1] * lv[1:2], axis=-1, keepdims=True))
           - jnp.exp(jnp.sum(lv[2:3] * lv[3:4], axis=-1, keepdims=True)) + lam_init)
    d = o[:t] - lam * o[t:]
    o_ref[0] = _rms(d, g_ref[...]) * (1.0 - lam_init)


def diff_attention_prompt(z, rel_bias, lamv, subln_g, lam_init):
    b, t_len, _ = z.shape
    t = ATTN_TILE
    assert t_len % t == 0
    w = 2 * HD_A
    return pl.pallas_call(
        functools.partial(_diff_attn_kernel, lam_init=lam_init),
        grid=(b, H_A, t_len // t),
        in_specs=[pl.BlockSpec(memory_space=pltpu.SMEM),
                  pl.BlockSpec((4, HD_A), lambda bi, h, qi: (0, 0)),
                  pl.BlockSpec((1, w), lambda bi, h, qi: (0, 0)),
                  pl.BlockSpec((1, t, w), lambda bi, h, qi: (bi, qi, COL_QA // w + h)),
                  pl.BlockSpec((1, t_len, w), lambda bi, h, qi: (bi, 0, COL_KA // w + h)),
                  pl.BlockSpec((1, t_len, w), lambda bi, h, qi: (bi, 0, COL_VA // w + h))],
        out_specs=pl.BlockSpec((1, t, w), lambda bi, h, qi: (bi, qi, h)),
        out_shape=jax.ShapeDtypeStruct((b, t_len, W_A), F32),
        scratch_shapes=[pltpu.VMEM((2, 2 * t, t), F32),
                        pltpu.VMEM((t_len, w), BF16),
                        pltpu.VMEM((t_len, 2 * w), BF16),
                        pltpu.VMEM((2 * t, w), BF16),
                        pltpu.VMEM((2 * t, LANES), F32),
                        pltpu.VMEM((2 * t, 2 * w), F32)],
        compiler_params=_params(("arbitrary", "arbitrary", "arbitrary")),
        name="diff_attn_prompt",
    )(rel_bias, lamv, subln_g.reshape(1, w), z, z, z)


def _moba_attn_kernel(rel_ref, q_ref, k_ref, v_ref, o_ref,
                      bias_ref, kb_ref, va_ref, kmean_ref, sel_ref, qs_ref, m_ref, acc_ref, *, nb):
    t = ATTN_TILE
    blk_per_tile = t // MOBA_BLOCK
    w = 2 * HD_B
    hp = pl.program_id(1)
    qi = pl.program_id(2)

    @pl.when(qi == 0)
    def _():
        _fill_tile_bias(bias_ref, rel_ref, (2 * H_A + 2 * hp, 2 * H_A + 2 * hp + 1), t)
        _prepare_kv(k_ref, v_ref, kb_ref, va_ref, t)
        kmean_ref[...] = jnp.zeros(kmean_ref.shape, F32)

        def mean_body(n, carry):
            start = pl.multiple_of(n * MOBA_BLOCK, MOBA_BLOCK)
            kmean_ref[pl.ds(n, 1), :] = jnp.mean(k_ref[0, pl.ds(start, MOBA_BLOCK), :], axis=0, keepdims=True)
            return carry

        lax.fori_loop(0, nb, mean_body, 0)

    qf = _stack_streams(q_ref[0], HD_B)
    qs_ref[...] = (qf * (HD_B ** -0.5 * LOG2E)).astype(BF16)
    gate = _dot_nt(qf, kmean_ref[...], precision=lax.Precision.HIGHEST)
    blk = lax.broadcasted_iota(jnp.int32, gate.shape, 1)
    row_in_tile = lax.broadcasted_iota(jnp.int32, gate.shape, 0) % t
    own = qi * blk_per_tile + row_in_tile // MOBA_BLOCK
    sel_ref[...] = jnp.where(_top_k_mask(gate, blk < own, MOBA_TOPK), 1.0, 0.0)
    m_ref[...] = jnp.full(m_ref.shape, NEG, F32)
    acc_ref[...] = jnp.zeros(acc_ref.shape, F32)

    def step(kb, bias, diagonal):
        keys = pl.ds(pl.multiple_of(kb * t, t), t)
        for c in range(2 * t // ATTN_ROW_CHUNK):
            rows = slice(c * ATTN_ROW_CHUNK, (c + 1) * ATTN_ROW_CHUNK)
            own_j = (c * ATTN_ROW_CHUNK % t) // MOBA_BLOCK
            s = _dot_nt(qs_ref[rows, :], kb_ref[keys, :])
            if bias is not None:
                s = s + bias[rows, :]
            sel = sel_ref[rows, :]
            lane = lax.broadcasted_iota(jnp.int32, sel.shape, 1)
            parts = []
            for j in range(blk_per_tile):
                sj = s[:, j * MOBA_BLOCK:(j + 1) * MOBA_BLOCK]
                if diagonal and j == own_j:
                    parts.append(sj)
                else:
                    picked = jnp.sum(jnp.where(lane == kb * blk_per_tile + j, sel, 0.0), axis=-1, keepdims=True)
                    parts.append(jnp.where(picked > 0.5, sj, NEG))
            _flash_step(jnp.concatenate(parts, axis=-1), va_ref[keys, :], m_ref.at[rows, :], acc_ref.at[rows, :])

    _far_blocks(jnp.maximum(qi - 1, 0), lambda kb: step(kb, None, False))

    @pl.when(qi >= 1)
    def _():
        step(qi - 1, bias_ref[0], False)

    step(qi, bias_ref[1], True)

    acc = acc_ref[...]
    o = acc[:, :w] / acc[:, w:]
    lane = lax.broadcasted_iota(jnp.int32, (t, w), 1)
    o_ref[0] = jnp.where(lane < HD_B, o[:t], o[t:])


def moba_attention_prompt(z, rel_bias):
    b, t_len, _ = z.shape
    t = ATTN_TILE
    assert t_len % t == 0 and t % MOBA_BLOCK == 0 and ATTN_ROW_CHUNK == MOBA_BLOCK
    nb = t_len // MOBA_BLOCK
    assert nb <= LANES
    w = 2 * HD_B
    return pl.pallas_call(
        functools.partial(_moba_attn_kernel, nb=nb),
        grid=(b, H_B // 2, t_len // t),
        in_specs=[pl.BlockSpec(memory_space=pltpu.SMEM),
                  pl.BlockSpec((1, t, w), lambda bi, hp, qi: (bi, qi, COL_QB // w + hp)),
                  pl.BlockSpec((1, t_len, w), lambda bi, hp, qi: (bi, 0, COL_KB // w + hp)),
                  pl.BlockSpec((1, t_len, w), lambda bi, hp, qi: (bi, 0, COL_VB // w + hp))],
        out_specs=pl.BlockSpec((1, t, w), lambda bi, hp, qi: (bi, qi, hp)),
        out_shape=jax.ShapeDtypeStruct((b, t_len, W_B), F32),
        scratch_shapes=[pltpu.VMEM((2, 2 * t, t), F32),
                        pltpu.VMEM((t_len, w), BF16),
                        pltpu.VMEM((t_len, 2 * w), BF16),
                        pltpu.VMEM((LANES, w), F32),
                        pltpu.VMEM((2 * t, LANES), F32),
                        pltpu.VMEM((2 * t, w), BF16),
                        pltpu.VMEM((2 * t, LANES), F32),
                        pltpu.VMEM((2 * t, 2 * w), F32)],
        compiler_params=_params(("arbitrary", "arbitrary", "arbitrary")),
        name="moba_attn_prompt",
    )(rel_bias, z, z, z)


def _masked_query_rows(q, lane_starts, width):
    lane = lax.broadcasted_iota(jnp.int32, q.shape, 1)
    return jnp.concatenate(
        [jnp.where(jnp.logical_and(lane >= s, lane < s + width), q, 0.0) for s in lane_starts], axis=0)


def _group_bias(dist, rel_ref, cols):
    return jnp.concatenate([_rel_bias(dist, rel_ref, col) for col in cols], axis=0)


def _pad_to_page(rows):
    pad = jnp.zeros((PAGE_SIZE - rows.shape[0], rows.shape[1]), rows.dtype)
    return jnp.concatenate([rows, pad], axis=0).astype(BF16)


def _diff_decode_kernel(pt_ref, rel_ref, lamv_ref, g_ref, q_ref, kn_ref, vn_ref, *rest, lam_init):
    del pt_ref
    npg = PAGES_PER_STEP
    kp_refs, vp_refs = rest[:npg], rest[npg:2 * npg]
    o_ref, wq_ref, m_ref, l_ref, acc_ref = rest[2 * npg:]
    c = pl.program_id(1)
    last = pl.num_programs(1) - 1
    r_new = q_ref.shape[1]
    w = 2 * HD_A
    grp = 2 * r_new
    rows = H_A * grp
    flat = PAGE_SIZE * H_A
    cols = [mp * H_A + h for h in range(H_A) for mp in range(2)]

    @pl.when(c == 0)
    def _():
        q = q_ref[0] * (HD_A ** -0.5)
        lane = lax.broadcasted_iota(jnp.int32, (r_new, w), 1)
        parts = []
        for h in range(H_A):
            qh = q[:, h * w:(h + 1) * w]
            parts += [jnp.where(lane < HD_A, qh, 0.0), jnp.where(lane >= HD_A, qh, 0.0)]
        wq_ref[...] = jnp.concatenate(parts, axis=0).astype(BF16)
        m_ref[...] = jnp.full(m_ref.shape, NEG, F32)
        l_ref[...] = jnp.zeros(l_ref.shape, F32)
        acc_ref[...] = jnp.zeros(acc_ref.shape, F32)

    q_head = lax.broadcasted_iota(jnp.int32, (rows, flat), 0) // grp
    k_head = lax.broadcasted_iota(jnp.int32, (rows, flat), 1) % H_A
    head_ok = q_head == k_head

    s_pages = []
    for p in range(npg):
        kf = kp_refs[p][...].reshape(flat, w).astype(BF16)
        s = _dot_nt(wq_ref[...], kf)
        if p == npg - 1:
            r = lax.broadcasted_iota(jnp.int32, (r_new, flat), 0)
            j = lax.broadcasted_iota(jnp.int32, (r_new, flat), 1) // H_A
            dist = jnp.where(c == last, r + PAGE_SIZE - j, FAR_DISTANCE)
            s = s + _group_bias(dist, rel_ref, cols)
        s_pages.append(jnp.where(head_ok, s, NEG))
    m_prev = m_ref[...]
    m_new = m_prev
    for s in s_pages:
        m_new = jnp.maximum(m_new, jnp.max(s, axis=-1, keepdims=True))
    alpha = jnp.exp(m_prev - m_new)
    l_new = alpha * l_ref[...]
    acc = alpha * acc_ref[...]
    for p, s in enumerate(s_pages):
        pexp = jnp.exp(s - m_new)
        l_new = l_new + jnp.sum(pexp, axis=-1, keepdims=True)
        vf = vp_refs[p][...].reshape(flat, w).astype(BF16)
        acc = acc + jnp.dot(pexp.astype(BF16), vf, preferred_element_type=F32)
    m_ref[...] = m_new
    l_ref[...] = l_new
    acc_ref[...] = acc

    @pl.when(c == last)
    def _():
        kn = jnp.concatenate([kn_ref[0, :, h * w:(h + 1) * w] for h in range(H_A)], axis=0)
        vn = jnp.concatenate([vn_ref[0, :, h * w:(h + 1) * w] for h in range(H_A)], axis=0)
        r = lax.broadcasted_iota(jnp.int32, (r_new, PAGE_SIZE), 0)
        j = lax.broadcasted_iota(jnp.int32, (r_new, PAGE_SIZE), 1) % r_new
        s = _dot_nt(wq_ref[...], _pad_to_page(kn)) + _group_bias(r - j, rel_ref, cols)
        qr = lax.broadcasted_iota(jnp.int32, s.shape, 0)
        kc = lax.broadcasted_iota(jnp.int32, s.shape, 1)
        ok = jnp.logical_and(kc // r_new == qr // grp, kc % r_new <= qr % r_new)
        _online_softmax_step(jnp.where(ok, s, NEG), _pad_to_page(vn), m_ref, l_ref, acc_ref)

        o = acc_ref[...] / l_ref[...]
        lv = lamv_ref[...]
        lam = (jnp.exp(jnp.sum(lv[0:1] * lv[1:2], axis=-1, keepdims=True))
               - jnp.exp(jnp.sum(lv[2:3] * lv[3:4], axis=-1, keepdims=True)) + lam_init)
        for h in range(H_A):
            o1 = o[h * grp:h * grp + r_new]
            o2 = o[h * grp + r_new:(h + 1) * grp]
            o_ref[0, :, h * w:(h + 1) * w] = _rms(o1 - lam * o2, g_ref[...]) * (1.0 - lam_init)


def diff_attention_decode(z, cache_k, cache_v, page_table, layer, rel_bias, lamv, subln_g, lam_init):
    n_seq, r_new, _ = z.shape
    n_pages = page_table.shape[1]
    assert n_pages % PAGES_PER_STEP == 0 and r_new == SUBLANES and H_A * r_new <= PAGE_SIZE
    w = 2 * HD_A
    rows = 2 * H_A * r_new
    seq_block = lambda col: pl.BlockSpec((1, r_new, W_A), lambda b, c, pt: (b, 0, col // W_A))
    page_specs = [pl.BlockSpec((None, PAGE_SIZE, None, H_A, w),
                               lambda b, c, pt, p=p: (pt[b, c * PAGES_PER_STEP + p], 0, layer, 0, 0))
                  for p in range(PAGES_PER_STEP)]
    grid_spec = pltpu.PrefetchScalarGridSpec(
        num_scalar_prefetch=1,
        grid=(n_seq, n_pages // PAGES_PER_STEP),
        in_specs=[pl.BlockSpec(memory_space=pltpu.SMEM),
                  pl.BlockSpec((4, HD_A), lambda b, c, pt: (0, 0)),
                  pl.BlockSpec((1, w), lambda b, c, pt: (0, 0)),
                  seq_block(COL_QA), seq_block(COL_KA), seq_block(COL_VA)] + page_specs + page_specs,
        out_specs=pl.BlockSpec((1, r_new, W_A), lambda b, c, pt: (b, 0, 0)),
        scratch_shapes=[pltpu.VMEM((rows, w), BF16),
                        pltpu.VMEM((rows, 1), F32),
                        pltpu.VMEM((rows, 1), F32),
                        pltpu.VMEM((rows, w), F32)])
    return pl.pallas_call(
        functools.partial(_diff_decode_kernel, lam_init=lam_init),
        grid_spec=grid_spec,
        out_shape=jax.ShapeDtypeStruct((n_seq, r_new, W_A), F32),
        compiler_params=_params(("arbitrary", "arbitrary")),
        name="diff_attn_decode",
    )(page_table, rel_bias, lamv, subln_g.reshape(1, w), z, z, z,
      *([cache_k] * PAGES_PER_STEP), *([cache_v] * PAGES_PER_STEP))


def _new_rows_scores(wq, k_new, rel_ref, cols):
    r_new = k_new.shape[0]
    r = lax.broadcasted_iota(jnp.int32, (r_new, PAGE_SIZE), 0)
    j = lax.broadcasted_iota(jnp.int32, (r_new, PAGE_SIZE), 1)
    s = _dot_nt(wq, _pad_to_page(k_new)) + _group_bias(r - j, rel_ref, cols)
    q_row = lax.broadcasted_iota(jnp.int32, s.shape, 0) % r_new
    return jnp.where(lax.broadcasted_iota(jnp.int32, s.shape, 1) <= q_row, s, NEG)


def _moba_decode_kernel(pt_ref, rel_ref, q_ref, kn_ref, vn_ref, *rest):
    del pt_ref
    npg = PAGES_PER_STEP
    kp_refs, vp_refs = rest[:npg], rest[npg:2 * npg]
    o_ref, wq_ref, wqf_ref, kmean_ref, mblk_ref, lblk_ref, accblk_ref = rest[2 * npg:]
    c = pl.program_id(1)
    last = pl.num_programs(1) - 1
    r_new = q_ref.shape[1]
    nb = accblk_ref.shape[0]
    pages_per_block = MOBA_BLOCK // PAGE_SIZE
    blocks_per_step = npg // pages_per_block
    cols = [2 * H_A + h for h in range(H_B)]

    @pl.when(c == 0)
    def _():
        qf = _masked_query_rows(q_ref[0], [h * HD_B for h in range(H_B)], HD_B)
        wqf_ref[...] = qf
        wq_ref[...] = (qf * (HD_B ** -0.5)).astype(BF16)
        kmean_ref[...] = jnp.zeros(kmean_ref.shape, F32)
        mblk_ref[...] = jnp.full(mblk_ref.shape, NEG, F32)
        lblk_ref[...] = jnp.zeros(lblk_ref.shape, F32)

    blk_iota = lax.broadcasted_iota(jnp.int32, mblk_ref.shape, 1)
    mean_lane = lax.broadcasted_iota(jnp.int32, kmean_ref.shape, 1)
    for jb in range(blocks_per_step):
        n = c * blocks_per_step + jb
        pages = range(jb * pages_per_block, (jb + 1) * pages_per_block)
        kts = [kp_refs[p][...].reshape(W_B, PAGE_SIZE) for p in pages]
        s_parts = [jnp.dot(wq_ref[...], kt.astype(BF16), preferred_element_type=F32) for kt in kts]
        if jb == blocks_per_step - 1:
            r = lax.broadcasted_iota(jnp.int32, (r_new, PAGE_SIZE), 0)
            j = lax.broadcasted_iota(jnp.int32, (r_new, PAGE_SIZE), 1)
            dist = jnp.where(c == last, r + PAGE_SIZE - j, FAR_DISTANCE)
            s_parts[-1] = s_parts[-1] + _group_bias(dist, rel_ref, cols)
        s = jnp.concatenate(s_parts, axis=-1)
        m = jnp.max(s, axis=-1, keepdims=True)
        pexp = jnp.exp(s - m)
        acc = sum(_dot_nt(pexp[:, i * PAGE_SIZE:(i + 1) * PAGE_SIZE].astype(BF16),
                          vp_refs[p][...].reshape(W_B, PAGE_SIZE).astype(BF16))
                  for i, p in enumerate(pages))
        accblk_ref[n] = acc
        mblk_ref[...] = jnp.where(blk_iota == n, m, mblk_ref[...])
        lblk_ref[...] = jnp.where(blk_iota == n, jnp.sum(pexp, axis=-1, keepdims=True), lblk_ref[...])
        kmean = sum(jnp.sum(kt, axis=-1, keepdims=True) for kt in kts) * (1.0 / MOBA_BLOCK)
        kmean_ref[...] = jnp.where(mean_lane == n, kmean, kmean_ref[...])

    @pl.when(c == last)
    def _():
        gate = jnp.dot(wqf_ref[...], kmean_ref[...], precision=lax.Precision.HIGHEST,
                       preferred_element_type=F32)
        sel = _top_k_mask(gate, blk_iota < nb, MOBA_TOPK)
        s_own = _new_rows_scores(wq_ref[...], kn_ref[0], rel_ref, cols)
        mblk = jnp.where(sel, mblk_ref[...], NEG)
        m_all = jnp.maximum(jnp.max(mblk, axis=-1, keepdims=True), jnp.max(s_own, axis=-1, keepdims=True))
        wgt = jnp.where(sel, jnp.exp(mblk - m_all), 0.0)
        p_own = jnp.exp(s_own - m_all)
        denom = (jnp.sum(wgt * lblk_ref[...], axis=-1, keepdims=True)
                 + jnp.sum(p_own, axis=-1, keepdims=True))
        acc = jnp.dot(p_own.astype(BF16), _pad_to_page(vn_ref[0]), preferred_element_type=F32)
        for n in range(nb):
            acc = acc + wgt[:, n:n + 1] * accblk_ref[n]
        o = acc / denom
        lane = lax.broadcasted_iota(jnp.int32, (r_new, W_B), 1)
        out = jnp.zeros((r_new, W_B), F32)
        for h in range(H_B):
            in_head = jnp.logical_and(lane >= h * HD_B, lane < (h + 1) * HD_B)
            out = out + jnp.where(in_head, o[h * r_new:(h + 1) * r_new, :], 0.0)
        o_ref[0] = out


def moba_attention_decode(z, cache_kt, cache_vt, page_table, layer, rel_bias):
    n_seq, r_new, _ = z.shape
    n_pages = page_table.shape[1]
    past = n_pages * PAGE_SIZE
    assert n_pages % PAGES_PER_STEP == 0 and past % MOBA_BLOCK == 0 and r_new == SUBLANES
    assert PAGES_PER_STEP % (MOBA_BLOCK // PAGE_SIZE) == 0 and r_new <= MOBA_BLOCK
    nb = past // MOBA_BLOCK
    assert nb <= LANES
    rows = H_B * r_new
    seq_block = lambda col: pl.BlockSpec((1, r_new, W_B), lambda b, c, pt: (b, 0, col // W_B))
    page_specs = [pl.BlockSpec((None, None, H_B, HD_B, PAGE_SIZE),
                               lambda b, c, pt, p=p: (pt[b, c * PAGES_PER_STEP + p], layer, 0, 0, 0))
                  for p in range(PAGES_PER_STEP)]
    grid_spec = pltpu.PrefetchScalarGridSpec(
        num_scalar_prefetch=1,
        grid=(n_seq, n_pages // PAGES_PER_STEP),
        in_specs=[pl.BlockSpec(memory_space=pltpu.SMEM),
                  seq_block(COL_QB), seq_block(COL_KB), seq_block(COL_VB)] + page_specs + page_specs,
        out_specs=pl.BlockSpec((1, r_new, W_B), lambda b, c, pt: (b, 0, 0)),
        scratch_shapes=[pltpu.VMEM((rows, W_B), BF16),
                        pltpu.VMEM((rows, W_B), F32),
                        pltpu.VMEM((W_B, LANES), F32),
                        pltpu.VMEM((rows, LANES), F32),
                        pltpu.VMEM((rows, LANES), F32),
                        pltpu.VMEM((nb, rows, W_B), F32)])
    return pl.pallas_call(
        _moba_decode_kernel,
        grid_spec=grid_spec,
        out_shape=jax.ShapeDtypeStruct((n_seq, r_new, W_B), F32),
        compiler_params=_params(("arbitrary", "arbitrary")),
        name="moba_attn_decode",
    )(page_table, rel_bias, z, z, z, *([cache_kt] * PAGES_PER_STEP), *([cache_vt] * PAGES_PER_STEP))


def _sigmoid(x):
    return 1.0 / (1.0 + jnp.exp(-x))


def _mix_out_kernel(oa_ref, ob_ref, ga_ref, gb_ref, x_ref, wa_ref, wb_ref, wo_ref, o_ref):
    ya = jnp.dot(oa_ref[...].astype(BF16), wa_ref[...], preferred_element_type=F32)
    yb = jnp.dot(ob_ref[...].astype(BF16), wb_ref[...], preferred_element_type=F32)
    mixed = _sigmoid(ga_ref[...]) * ya + _sigmoid(gb_ref[...]) * yb
    o_ref[...] = x_ref[...] + jnp.dot(mixed.astype(BF16), wo_ref[...], preferred_element_type=F32)


def mix_out(oa, ob, z, x, wa, wb, wo):
    n, d = x.shape
    tm = min(MIX_ROW_TILE, n)
    assert n % tm == 0
    full = lambda a: pl.BlockSpec(a.shape, lambda i: (0, 0))
    return pl.pallas_call(
        _mix_out_kernel,
        grid=(n // tm,),
        in_specs=[pl.BlockSpec((tm, W_A), lambda i: (i, 0)),
                  pl.BlockSpec((tm, W_B), lambda i: (i, 0)),
                  pl.BlockSpec((tm, d), lambda i: (i, COL_GA // d)),
                  pl.BlockSpec((tm, d), lambda i: (i, COL_GB // d)),
                  pl.BlockSpec((tm, d), lambda i: (i, 0)),
                  full(wa), full(wb), full(wo)],
        out_specs=pl.BlockSpec((tm, d), lambda i: (i, 0)),
        out_shape=jax.ShapeDtypeStruct((n, d), F32),
        compiler_params=_params(("arbitrary",)),
        name="mix_out",
    )(oa, ob, z, z, x, wa, wb, wo)


def _mem_attn_kernel(x_ref, g_ref, k_ref, v_ref, wq_ref, wo_ref, o_ref):
    x = x_ref[0]
    q = jnp.dot(_rms(x, g_ref[...]).astype(BF16), wq_ref[...], preferred_element_type=F32)
    heads = []
    for h in range(H_M):
        sl = slice(h * HD_M, (h + 1) * HD_M)
        s = _dot_nt((q[:, sl] * (HD_M ** -0.5)).astype(BF16), k_ref[0, :, sl].astype(BF16))
        p = jnp.exp(s - jnp.max(s, axis=-1, keepdims=True))
        oh = jnp.dot(p.astype(BF16), v_ref[0, :, sl].astype(BF16), preferred_element_type=F32)
        heads.append(oh / jnp.sum(p, axis=-1, keepdims=True))
    o = jnp.concatenate(heads, axis=-1)
    o_ref[0] = x + jnp.dot(o.astype(BF16), wo_ref[...], preferred_element_type=F32)


def memory_attention(x, g, mem_k, k_col, mem_v, v_col, wq, wo):
    grp, t_len, d = x.shape
    n_mem = mem_k.shape[1]
    tm = min(MIX_ROW_TILE, t_len)
    assert t_len % tm == 0
    return pl.pallas_call(
        _mem_attn_kernel,
        grid=(grp, t_len // tm),
        in_specs=[pl.BlockSpec((1, tm, d), lambda b, i: (b, i, 0)),
                  pl.BlockSpec((1, d), lambda b, i: (0, 0)),
                  pl.BlockSpec((1, n_mem, W_M), lambda b, i: (b, 0, k_col)),
                  pl.BlockSpec((1, n_mem, W_M), lambda b, i: (b, 0, v_col)),
                  pl.BlockSpec(wq.shape, lambda b, i: (0, 0)),
                  pl.BlockSpec(wo.shape, lambda b, i: (0, 0))],
        out_specs=pl.BlockSpec((1, tm, d), lambda b, i: (b, i, 0)),
        out_shape=jax.ShapeDtypeStruct((grp, t_len, d), F32),
        compiler_params=_params(("arbitrary", "arbitrary")),
        name="memory_attention",
    )(x, g.reshape(1, d), mem_k, mem_v, wq, wo)


def fuse_gate_up(w_gate, w_up):
    def tiles(w):
        *lead, d, f = w.shape
        assert f % FF_TILE == 0
        return jnp.moveaxis(w.astype(BF16).reshape(*lead, d, f // FF_TILE, FF_TILE), -2, -3)

    return jnp.concatenate([tiles(w_gate), tiles(w_up)], axis=-1)


def _swiglu_hidden(xn, wgu):
    tf = wgu.shape[1] // 2
    a = jnp.dot(xn, wgu, preferred_element_type=F32)
    gate = a[:, :tf]
    return gate * _sigmoid(gate) * a[:, tf:]


def _dense_ffn_kernel(x_ref, g_ref, wgu_ref, wd_ref, o_ref, xn_ref, acc_ref):
    j = pl.program_id(1)

    @pl.when(j == 0)
    def _():
        xn_ref[...] = _rms(x_ref[...], g_ref[...]).astype(BF16)
        acc_ref[...] = jnp.zeros(acc_ref.shape, F32)

    hid = _swiglu_hidden(xn_ref[...], wgu_ref[0])
    acc_ref[...] += jnp.dot(hid.astype(BF16), wd_ref[...], preferred_element_type=F32)

    @pl.when(j == pl.num_programs(1) - 1)
    def _():
        o_ref[...] = x_ref[...] + acc_ref[...]


def dense_ffn(x, g, wgu, wd):
    n, d = x.shape
    n_f, _, tf2 = wgu.shape
    tf = tf2 // 2
    tm = min(ROW_TILE, n)
    assert n % tm == 0 and wd.shape[0] == n_f * tf
    return pl.pallas_call(
        _dense_ffn_kernel,
        grid=(n // tm, n_f),
        in_specs=[pl.BlockSpec((tm, d), lambda i, j: (i, 0)),
                  pl.BlockSpec((1, d), lambda i, j: (0, 0)),
                  pl.BlockSpec((1, d, tf2), lambda i, j: (j, 0, 0)),
                  pl.BlockSpec((tf, d), lambda i, j: (j, 0))],
        out_specs=pl.BlockSpec((tm, d), lambda i, j: (i, 0)),
        out_shape=jax.ShapeDtypeStruct((n, d), F32),
        scratch_shapes=[pltpu.VMEM((tm, d), BF16), pltpu.VMEM((tm, d), F32)],
        compiler_params=_params(("arbitrary", "arbitrary")),
        name="dense_ffn",
    )(x, g.reshape(1, d), wgu, wd)


def _moe_ffn_kernel(x_ref, g_ref, r_ref, wgu_ref, wd_ref, o_ref,
                    xn_ref, gate_ref, rank_ref, rankt_ref, membt_ref, xc_ref, yc_ref, acc_ref, cnt_ref, *, cap):
    e = pl.program_id(1)
    j = pl.program_id(2)
    last_j = pl.num_programs(2) - 1
    tm = x_ref.shape[0]

    @pl.when(jnp.logical_and(e == 0, j == 0))
    def _():
        xn = _rms(x_ref[...], g_ref[...])
        xn_ref[...] = xn.astype(BF16)
        acc_ref[...] = jnp.zeros(acc_ref.shape, F32)
        logits = jnp.dot(xn, r_ref[...], precision=lax.Precision.HIGHEST, preferred_element_type=F32)
        lane = lax.broadcasted_iota(jnp.int32, logits.shape, 1).astype(F32)
        logits = jnp.where(lane < N_EXPERTS, logits, -jnp.inf)
        v1 = jnp.max(logits, axis=-1, keepdims=True)
        i1 = jnp.min(jnp.where(logits == v1, lane, float(LANES)), axis=-1, keepdims=True)
        rest = jnp.where(lane == i1, -jnp.inf, logits)
        v2 = jnp.max(rest, axis=-1, keepdims=True)
        i2 = jnp.min(jnp.where(rest == v2, lane, float(LANES)), axis=-1, keepdims=True)
        e2 = jnp.exp(v2 - v1)
        denom = 1.0 + e2
        gate_ref[...] = jnp.where(lane == i1, 1.0 / denom, 0.0) + jnp.where(lane == i2, e2 / denom, 0.0)
        memb = jnp.where(gate_ref[...] > 0.0, 1.0, 0.0)
        r = lax.broadcasted_iota(jnp.int32, (tm, tm), 0)
        c = lax.broadcasted_iota(jnp.int32, (tm, tm), 1)
        rank_ref[...] = jnp.dot(jnp.where(c < r, 1.0, 0.0).astype(BF16), memb.astype(BF16),
                                preferred_element_type=F32)
        membt = memb.T
        membt_ref[...] = membt
        rankt_ref[...] = jnp.dot(membt.astype(BF16), jnp.where(r < c, 1.0, 0.0).astype(BF16),
                                 preferred_element_type=F32)

    def expert_column(ref):
        lane = lax.broadcasted_iota(jnp.int32, ref.shape, 1)
        return jnp.sum(jnp.where(lane == e, ref[...], 0.0), axis=-1, keepdims=True)

    @pl.when(j == 0)
    def _():
        cnt_ref[0] = jnp.sum(jnp.where(expert_column(gate_ref) > 0.0, 1.0, 0.0)).astype(jnp.int32)

    n_chunks = (cnt_ref[0] + (cap - 1)) // cap

    @pl.when(j == 0)
    def _():
        rt = rankt_ref[pl.ds(e, 1), :]
        mt = membt_ref[pl.ds(e, 1), :]

        def gather(ci, carry):
            base = pl.multiple_of(ci * cap, cap)
            slot = (lax.broadcasted_iota(jnp.int32, (cap, tm), 0) + base).astype(F32)
            onehot = jnp.where(jnp.logical_and(slot == rt, mt > 0.5), 1.0, 0.0).astype(BF16)
            xc_ref[pl.ds(base, cap), :] = jnp.dot(onehot, xn_ref[...], preferred_element_type=F32).astype(BF16)
            yc_ref[pl.ds(base, cap), :] = jnp.zeros((cap, yc_ref.shape[1]), F32)
            return carry

        lax.fori_loop(0, n_chunks, gather, 0)

    def ffn(ci, carry):
        rows = pl.ds(pl.multiple_of(ci * cap, cap), cap)
        hid = _swiglu_hidden(xc_ref[rows, :], wgu_ref[0, 0])
        yc_ref[rows, :] += jnp.dot(hid.astype(BF16), wd_ref[0], preferred_element_type=F32)
        return carry

    lax.fori_loop(0, n_chunks, ffn, 0)

    @pl.when(j == last_j)
    def _():
        rank_e = expert_column(rank_ref)
        gate_e = expert_column(gate_ref)

        def scatter(ci, carry):
            base = pl.multiple_of(ci * cap, cap)
            slot = (lax.broadcasted_iota(jnp.int32, (tm, cap), 1) + base).astype(F32)
            onehot_t = jnp.where(slot == rank_e, gate_e, 0.0).astype(BF16)
            acc_ref[...] += jnp.dot(onehot_t, yc_ref[pl.ds(base, cap), :].astype(BF16), preferred_element_type=F32)
            return carry

        lax.fori_loop(0, n_chunks, scatter, 0)

    @pl.when(jnp.logical_and(e == pl.num_programs(1) - 1, j == last_j))
    def _():
        o_ref[...] = x_ref[...] + acc_ref[...]


def moe_ffn(x, g, router_padded, wgu, wd):
    n, d = x.shape
    n_exp, n_f, _, tf2 = wgu.shape
    tf = tf2 // 2
    tm = min(ROW_TILE, n)
    cap = min(MOE_CHUNK, tm)
    cap_rows = -(-tm // cap) * cap
    assert n % tm == 0 and wd.shape[1] == n_f * tf and n_exp == N_EXPERTS
    assert tm % LANES == 0 and cap % (2 * SUBLANES) == 0
    return pl.pallas_call(
        functools.partial(_moe_ffn_kernel, cap=cap),
        grid=(n // tm, n_exp, n_f),
        in_specs=[pl.BlockSpec((tm, d), lambda i, e, j: (i, 0)),
                  pl.BlockSpec((1, d), lambda i, e, j: (0, 0)),
                  pl.BlockSpec((d, LANES), lambda i, e, j: (0, 0)),
                  pl.BlockSpec((1, 1, d, tf2), lambda i, e, j: (e, j, 0, 0)),
                  pl.BlockSpec((1, tf, d), lambda i, e, j: (e, j, 0))],
        out_specs=pl.BlockSpec((tm, d), lambda i, e, j: (i, 0)),
        out_shape=jax.ShapeDtypeStruct((n, d), F32),
        scratch_shapes=[pltpu.VMEM((tm, d), BF16),
                        pltpu.VMEM((tm, LANES), F32),
                        pltpu.VMEM((tm, LANES), F32),
                        pltpu.VMEM((LANES, tm), F32),
                        pltpu.VMEM((LANES, tm), F32),
                        pltpu.VMEM((cap_rows, d), BF16),
                        pltpu.VMEM((cap_rows, d), F32),
                        pltpu.VMEM((tm, d), F32),
                        pltpu.SMEM((1,), jnp.int32)],
        compiler_params=_params(("arbitrary", "arbitrary", "arbitrary")),
        name="moe_ffn",
    )(x, g.reshape(1, d), router_padded, wgu, wd)


def kernel(x_prompt, x_sample, mem_prompt, cache_a_k, cache_a_v, cache_b_k, cache_b_v, cache_mem_k, cache_mem_v, page_table, rel_bias, norm_mix, w_in, lambda_q1, lambda_k1, lambda_q2, lambda_k2, subln_gain, w_br_a, w_br_b, w_out, norm_mem_q, norm_mem_kv, w_mem_q, w_mem_kv, w_mem_o, norm_ffn, ffn_w_gate, ffn_w_up, ffn_w_down, moe_router, moe_w_gate, moe_w_up, moe_w_down, norm_final):
    bsz, seq, d = x_prompt.shape
    n_seq, r_new, _ = x_sample.shape
    n_mem = mem_prompt.shape[1]
    depth = w_in.shape[0]

    cb_kt = jnp.transpose(cache_b_k, (0, 2, 3, 4, 1))
    cb_vt = jnp.transpose(cache_b_v, (0, 2, 3, 4, 1))
    cm_k = cache_mem_k.reshape(n_seq, n_mem, depth * W_M)
    cm_v = cache_mem_v.reshape(n_seq, n_mem, depth * W_M)

    hp = x_prompt.reshape(bsz * seq, d)
    hs = x_sample.reshape(n_seq * r_new, d)
    zs_p, zs_s, mkvs = [], [], []
    for l in range(depth):
        lam_init = 0.8 - 0.6 * math.exp(-0.3 * l)
        lamv = jnp.stack([lambda_q1[l], lambda_k1[l], lambda_q2[l], lambda_k2[l]]).astype(F32)
        w_in_l = w_in[l].astype(BF16)
        wa, wb, wo = w_br_a[l].astype(BF16), w_br_b[l].astype(BF16), w_out[l].astype(BF16)

        zp = norm_matmul(hp, norm_mix[l], w_in_l)
        zp3 = zp.reshape(bsz, seq, IN_COLS)
        oa = diff_attention_prompt(zp3, rel_bias, lamv, subln_gain[l], lam_init)
        ob = moba_attention_prompt(zp3, rel_bias)
        hp = mix_out(oa.reshape(bsz * seq, W_A), ob.reshape(bsz * seq, W_B), zp, hp, wa, wb, wo)
        zs_p.append(zp3)

        zs = norm_matmul(hs, norm_mix[l], w_in_l)
        zs3 = zs.reshape(n_seq, r_new, IN_COLS)
        oa_s = diff_attention_decode(zs3, cache_a_k, cache_a_v, page_table, l, rel_bias, lamv, subln_gain[l], lam_init)
        ob_s = moba_attention_decode(zs3, cb_kt, cb_vt, page_table, l, rel_bias)
        hs = mix_out(oa_s.reshape(n_seq * r_new, W_A), ob_s.reshape(n_seq * r_new, W_B), zs, hs, wa, wb, wo)
        zs_s.append(zs3)

        mkv = norm_matmul(mem_prompt.reshape(bsz * n_mem, d), norm_mem_kv[l], w_mem_kv[l].astype(BF16))
        mkv3 = mkv.reshape(bsz, n_mem, 2 * W_M)
        mkvs.append(mkv3)
        wq, wmo = w_mem_q[l].astype(BF16), w_mem_o[l].astype(BF16)
        hp = memory_attention(hp.reshape(bsz, seq, d), norm_mem_q[l], mkv3, 0, mkv3, 1, wq, wmo).reshape(bsz * seq, d)
        hs = memory_attention(hs.reshape(n_seq, r_new, d), norm_mem_q[l], cm_k, l, cm_v, l, wq, wmo).reshape(n_seq * r_new, d)

        i = l // 2
        if l % 2 == 0:
            wgu, wd = fuse_gate_up(ffn_w_gate[i], ffn_w_up[i]), ffn_w_down[i].astype(BF16)
            hp = dense_ffn(hp, norm_ffn[l], wgu, wd)
            hs = dense_ffn(hs, norm_ffn[l], wgu, wd)
        else:
            wgu, wd = fuse_gate_up(moe_w_gate[i], moe_w_up[i]), moe_w_down[i].astype(BF16)
            router = jnp.pad(moe_router[i].astype(F32), ((0, 0), (0, LANES - N_EXPERTS)))
            hp = moe_ffn(hp, norm_ffn[l], router, wgu, wd)
            hs = moe_ffn(hs, norm_ffn[l], router, wgu, wd)

    y_prompt = rmsnorm(hp, norm_final).reshape(bsz, seq, d)
    y_sample = rmsnorm(hs, norm_final).reshape(n_seq, r_new, d)

    def stacked(zs, col, width, heads):
        rows = jnp.stack([z[:, :, col:col + width] for z in zs], axis=2)
        return rows.reshape(rows.shape[0], rows.shape[1], depth, heads, width // heads)

    new_mem_k = jnp.stack([m[:, :, :W_M] for m in mkvs], axis=2).reshape(bsz, n_mem, depth, H_M, HD_M)
    new_mem_v = jnp.stack([m[:, :, W_M:] for m in mkvs], axis=2).reshape(bsz, n_mem, depth, H_M, HD_M)
    return (y_prompt, y_sample,
            stacked(zs_p, COL_KA, W_A, H_A), stacked(zs_p, COL_VA, W_A, H_A),
            stacked(zs_p, COL_KB, W_B, H_B), stacked(zs_p, COL_VB, W_B, H_B),
            new_mem_k, new_mem_v,
            stacked(zs_s, COL_KA, W_A, H_A), stacked(zs_s, COL_VA, W_A, H_A),
            stacked(zs_s, COL_KB, W_B, H_B), stacked(zs_s, COL_VB, W_B, H_B))
```

```python
import functools
import math

import numpy as np
import jax
import jax.numpy as jnp
from jax import lax
from jax.experimental import pallas as pl
from jax.experimental.pallas import tpu as pltpu

F32 = jnp.float32
BF16 = jnp.bfloat16

D_MODEL = 1024
DEPTH = 2
PAGE_SIZE = 128
H_A = 4
HD_A = 64
W_A = H_A * 2 * HD_A
H_B = 8
HD_B = 64
W_B = H_B * HD_B
MOBA_BLOCK = 256
MOBA_TOPK = 3
H_M = 4
HD_M = 128
W_M = H_M * HD_M
N_BUCKETS = 32
MAX_DISTANCE = 128
N_EXPERTS = 8
TOP_K_EXPERTS = 2
IN_COLS = 3 * W_A + 3 * W_B + 2 * D_MODEL
EPS = 1e-6
NEG = -1e30
LOG2E = math.log2(math.e)

LANES = 128
SUBLANES = 8
VMEM_LIMIT_BYTES = 56 * 1024 * 1024

ATTN_TILE = 2 * MOBA_BLOCK
ATTN_ROW_CHUNK = MOBA_BLOCK
ATTN_UNROLL = 4
PAGES_PER_STEP = 16
ROW_TILE = 1024
COL_TILE = 1024
FF_TILE = 256
MOE_CHUNK = 320
MOE_FF_TILE = 1408
MIX_ROW_TILE = 512

COL_QA, COL_KA, COL_VA = 0, W_A, 2 * W_A
COL_QB, COL_KB, COL_VB = 3 * W_A, 3 * W_A + W_B, 3 * W_A + 2 * W_B
COL_GA = 3 * W_A + 3 * W_B
COL_GB = COL_GA + D_MODEL


def _t5_bucket_starts():
    n = np.arange(0, 4 * MAX_DISTANCE, dtype=np.int64)
    max_exact = N_BUCKETS // 2
    nf = np.maximum(n, 1).astype(np.float32)
    large = max_exact + (np.log(nf / np.float32(max_exact))
                         / np.float32(math.log(MAX_DISTANCE / max_exact))
                         * np.float32(N_BUCKETS - max_exact)).astype(np.int32)
    bucket = np.where(n < max_exact, n, np.minimum(large, N_BUCKETS - 1))
    assert np.all(np.diff(bucket) >= 0) and bucket[-1] == N_BUCKETS - 1
    starts = [int(np.argmax(bucket >= b)) for b in range(N_BUCKETS)]
    assert starts[-1] <= MAX_DISTANCE
    return starts


BUCKET_START = _t5_bucket_starts()
FAR_DISTANCE = BUCKET_START[-1]


def _params(semantics):
    return pltpu.CompilerParams(dimension_semantics=semantics, vmem_limit_bytes=VMEM_LIMIT_BYTES)


def _rms(x, g):
    return x * lax.rsqrt(jnp.mean(x * x, axis=-1, keepdims=True) + EPS) * g


def _rel_bias(dist, rel_ref, col, scale=1.0):
    last = rel_ref[N_BUCKETS - 1, col]
    out = jnp.zeros(dist.shape, F32)
    for b in range(N_BUCKETS - 2, -1, -1):
        out = jnp.where(dist < BUCKET_START[b + 1], (rel_ref[b, col] - last) * scale, out)
    return out


def _dot_nt(a, b, precision=None):
    return lax.dot_general(a, b, (((1,), (1,)), ((), ())), precision=precision,
                           preferred_element_type=F32)


def _online_softmax_step(s, v, m_ref, l_ref, acc_ref):
    m_prev = m_ref[...]
    m_new = jnp.maximum(m_prev, jnp.max(s, axis=-1, keepdims=True))
    alpha = jnp.exp(m_prev - m_new)
    p = jnp.exp(s - m_new)
    l_ref[...] = alpha * l_ref[...] + jnp.sum(p, axis=-1, keepdims=True)
    acc_ref[...] = alpha * acc_ref[...] + jnp.dot(p.astype(BF16), v, preferred_element_type=F32)
    m_ref[...] = m_new


def _top_k_mask(gate, valid, k):
    idx = lax.broadcasted_iota(jnp.int32, gate.shape, 1).astype(F32)
    g = jnp.where(valid, gate, NEG)
    sel = jnp.zeros(gate.shape, F32)
    for _ in range(k):
        mx = jnp.max(g, axis=-1, keepdims=True)
        first = jnp.min(jnp.where(g == mx, idx, float(gate.shape[-1])), axis=-1, keepdims=True)
        pick = idx == first
        sel = jnp.where(pick, 1.0, sel)
        g = jnp.where(pick, -jnp.inf, g)
    return jnp.logical_and(sel > 0.5, valid)


def _norm_matmul_kernel(x_ref, g_ref, w_ref, o_ref, xn_ref):
    @pl.when(pl.program_id(1) == 0)
    def _():
        xn_ref[...] = _rms(x_ref[...], g_ref[...]).astype(BF16)

    o_ref[...] = jnp.dot(xn_ref[...], w_ref[...], preferred_element_type=F32)


def norm_matmul(x, g, w_bf16):
    n, d = x.shape
    c = w_bf16.shape[1]
    tm = min(ROW_TILE, n)
    tn = min(COL_TILE, c)
    assert n % tm == 0 and c % tn == 0
    return pl.pallas_call(
        _norm_matmul_kernel,
        grid=(n // tm, c // tn),
        in_specs=[pl.BlockSpec((tm, d), lambda i, j: (i, 0)),
                  pl.BlockSpec((1, d), lambda i, j: (0, 0)),
                  pl.BlockSpec((d, tn), lambda i, j: (0, j))],
        out_specs=pl.BlockSpec((tm, tn), lambda i, j: (i, j)),
        out_shape=jax.ShapeDtypeStruct((n, c), F32),
        scratch_shapes=[pltpu.VMEM((tm, d), BF16)],
        compiler_params=_params(("arbitrary", "arbitrary")),
        name="norm_matmul",
    )(x, g.reshape(1, d), w_bf16)


def _rmsnorm_kernel(x_ref, g_ref, o_ref):
    o_ref[...] = _rms(x_ref[...], g_ref[...])


def rmsnorm(x, g):
    n, d = x.shape
    tm = min(ROW_TILE, n)
    return pl.pallas_call(
        _rmsnorm_kernel,
        grid=(n // tm,),
        in_specs=[pl.BlockSpec((tm, d), lambda i: (i, 0)), pl.BlockSpec((1, d), lambda i: (0, 0))],
        out_specs=pl.BlockSpec((tm, d), lambda i: (i, 0)),
        out_shape=jax.ShapeDtypeStruct((n, d), F32),
        compiler_params=_params(("arbitrary",)),
        name="final_rmsnorm",
    )(x, g.reshape(1, d))


def _stack_streams(q, half):
    lane = lax.broadcasted_iota(jnp.int32, q.shape, 1)
    return jnp.concatenate([jnp.where(lane < half, q, 0.0), jnp.where(lane >= half, q, 0.0)], axis=0)


def _fill_tile_bias(bias_ref, rel_ref, cols, t):
    r = lax.broadcasted_iota(jnp.int32, (t, t), 0)
    c = lax.broadcasted_iota(jnp.int32, (t, t), 1)
    d_diag = r - c
    for s, col in enumerate(cols):
        bias_ref[0, s * t:(s + 1) * t, :] = _rel_bias(d_diag + t, rel_ref, col, LOG2E)
        bias_ref[1, s * t:(s + 1) * t, :] = jnp.where(d_diag >= 0, _rel_bias(d_diag, rel_ref, col, LOG2E), NEG)


def _prepare_kv(k_ref, v_ref, kb_ref, va_ref, t):
    n_blocks = k_ref.shape[1] // t
    w = v_ref.shape[2]

    def body(n, carry):
        rows = pl.ds(pl.multiple_of(n * t, t), t)
        kb_ref[rows, :] = k_ref[0, rows, :].astype(BF16)
        va_ref[rows, :w] = v_ref[0, rows, :].astype(BF16)
        va_ref[rows, w:] = jnp.ones((t, w), BF16)
        return carry

    lax.fori_loop(0, n_blocks, body, 0)


def _flash_step(s, va, m_ref, acc_ref):
    m_prev = m_ref[...]
    m_new = jnp.maximum(m_prev, jnp.max(s, axis=-1, keepdims=True))
    alpha = jnp.exp2(m_prev - m_new)
    p = jnp.exp2(s - jnp.concatenate([m_new] * (s.shape[1] // LANES), axis=1))
    acc_ref[...] = (jnp.concatenate([alpha] * (acc_ref.shape[1] // LANES), axis=1) * acc_ref[...]
                    + jnp.dot(p.astype(BF16), va, preferred_element_type=F32))
    m_ref[...] = m_new


def _far_blocks(n_far, step):
    def group(g, carry):
        for u in range(ATTN_UNROLL):
            step(ATTN_UNROLL * g + u)
        return carry

    def single(kb, carry):
        step(kb)
        return carry

    lax.fori_loop(0, n_far // ATTN_UNROLL, group, 0)
    lax.fori_loop(n_far - n_far % ATTN_UNROLL, n_far, single, 0)


def _diff_attn_kernel(rel_ref, lamv_ref, g_ref, q_ref, k_ref, v_ref, o_ref,
                      bias_ref, kb_ref, va_ref, qs_ref, m_ref, acc_ref, *, lam_init):
    t = ATTN_TILE
    w = 2 * HD_A
    h = pl.program_id(1)
    qi = pl.program_id(2)

    @pl.when(qi == 0)
    def _():
        _fill_tile_bias(bias_ref, rel_ref, (h, H_A + h), t)
        _prepare_kv(k_ref, v_ref, kb_ref, va_ref, t)

    qs_ref[...] = _stack_streams(q_ref[0] * (HD_A ** -0.5 * LOG2E), HD_A).astype(BF16)
    m_ref[...] = jnp.full(m_ref.shape, NEG, F32)
    acc_ref[...] = jnp.zeros(acc_ref.shape, F32)

    def step(kb, bias):
        keys = pl.ds(pl.multiple_of(kb * t, t), t)
        for c in range(2 * t // ATTN_ROW_CHUNK):
            rows = slice(c * ATTN_ROW_CHUNK, (c + 1) * ATTN_ROW_CHUNK)
            s = _dot_nt(qs_ref[rows, :], kb_ref[keys, :])
            if bias is not None:
                s = s + bias[rows, :]
            _flash_step(s, va_ref[keys, :], m_ref.at[rows, :], acc_ref.at[rows, :])

    _far_blocks(jnp.maximum(qi - 1, 0), lambda kb: step(kb, None))

    @pl.when(qi >= 1)
    def _():
        step(qi - 1, bias_ref[0])
        step(qi, bias_ref[1])

    @pl.when(qi == 0)
    def _():
        step(0, bias_ref[1])

    acc = acc_ref[...]
    o = acc[:, :w] / acc[:, w:]
    lv = lamv_ref[...]
    lam = (jnp.exp(jnp.sum(lv[0:1] * lv[1:2], axis=-1, keepdims=True))
           - jnp.exp(jnp.sum(lv[2:3] * lv[3:4], axis=-1, keepdims=True)) + lam_init)
    d = o[:t] - lam * o[t:]
    o_ref[0] = _rms(d, g_ref[...]) * (1.0 - lam_init)


def diff_attention_prompt(z, rel_bias, lamv, subln_g, lam_init):
    b, t_len, _ = z.shape
    t = ATTN_TILE
    assert t_len % t == 0
    w = 2 * HD_A
    return pl.pallas_call(
        functools.partial(_diff_attn_kernel, lam_init=lam_init),
        grid=(b, H_A, t_len // t),
        in_specs=[pl.BlockSpec(memory_space=pltpu.SMEM),
                  pl.BlockSpec((4, HD_A), lambda bi, h, qi: (0, 0)),
                  pl.BlockSpec((1, w), lambda bi, h, qi: (0, 0)),
                  pl.BlockSpec((1, t, w), lambda bi, h, qi: (bi, qi, COL_QA // w + h)),
                  pl.BlockSpec((1, t_len, w), lambda bi, h, qi: (bi, 0, COL_KA // w + h)),
                  pl.BlockSpec((1, t_len, w), lambda bi, h, qi: (bi, 0, COL_VA // w + h))],
        out_specs=pl.BlockSpec((1, t, w), lambda bi, h, qi: (bi, qi, h)),
        out_shape=jax.ShapeDtypeStruct((b, t_len, W_A), F32),
        scratch_shapes=[pltpu.VMEM((2, 2 * t, t), F32),
                        pltpu.VMEM((t_len, w), BF16),
                        pltpu.VMEM((t_len, 2 * w), BF16),
                        pltpu.VMEM((2 * t, w), BF16),
                        pltpu.VMEM((2 * t, LANES), F32),
                        pltpu.VMEM((2 * t, 2 * w), F32)],
        compiler_params=_params(("arbitrary", "arbitrary", "arbitrary")),
        name="diff_attn_prompt",
    )(rel_bias, lamv, subln_g.reshape(1, w), z, z, z)


def _moba_attn_kernel(rel_ref, q_ref, k_ref, v_ref, o_ref,
                      bias_ref, kb_ref, va_ref, kmean_ref, sel_ref, qs_ref, m_ref, acc_ref, *, nb):
    t = ATTN_TILE
    blk_per_tile = t // MOBA_BLOCK
    w = 2 * HD_B
    hp = pl.program_id(1)
    qi = pl.program_id(2)

    @pl.when(qi == 0)
    def _():
        _fill_tile_bias(bias_ref, rel_ref, (2 * H_A + 2 * hp, 2 * H_A + 2 * hp + 1), t)
        _prepare_kv(k_ref, v_ref, kb_ref, va_ref, t)
        kmean_ref[...] = jnp.zeros(kmean_ref.shape, F32)

        def mean_body(n, carry):
            start = pl.multiple_of(n * MOBA_BLOCK, MOBA_BLOCK)
            kmean_ref[pl.ds(n, 1), :] = jnp.mean(k_ref[0, pl.ds(start, MOBA_BLOCK), :], axis=0, keepdims=True)
            return carry

        lax.fori_loop(0, nb, mean_body, 0)

    qf = _stack_streams(q_ref[0], HD_B)
    qs_ref[...] = (qf * (HD_B ** -0.5 * LOG2E)).astype(BF16)
    gate = _dot_nt(qf, kmean_ref[...], precision=lax.Precision.HIGHEST)
    blk = lax.broadcasted_iota(jnp.int32, gate.shape, 1)
    row_in_tile = lax.broadcasted_iota(jnp.int32, gate.shape, 0) % t
    own = qi * blk_per_tile + row_in_tile // MOBA_BLOCK
    sel_ref[...] = jnp.where(_top_k_mask(gate, blk < own, MOBA_TOPK), 1.0, 0.0)
    m_ref[...] = jnp.full(m_ref.shape, NEG, F32)
    acc_ref[...] = jnp.zeros(acc_ref.shape, F32)

    def step(kb, bias, diagonal):
        keys = pl.ds(pl.multiple_of(kb * t, t), t)
        for c in range(2 * t // ATTN_ROW_CHUNK):
            rows = slice(c * ATTN_ROW_CHUNK, (c + 1) * ATTN_ROW_CHUNK)
            own_j = (c * ATTN_ROW_CHUNK % t) // MOBA_BLOCK
            s = _dot_nt(qs_ref[rows, :], kb_ref[keys, :])
            if bias is not None:
                s = s + bias[rows, :]
            sel = sel_ref[rows, :]
            lane = lax.broadcasted_iota(jnp.int32, sel.shape, 1)
            parts = []
            for j in range(blk_per_tile):
                sj = s[:, j * MOBA_BLOCK:(j + 1) * MOBA_BLOCK]
                if diagonal and j == own_j:
                    parts.append(sj)
                else:
                    picked = jnp.sum(jnp.where(lane == kb * blk_per_tile + j, sel, 0.0), axis=-1, keepdims=True)
                    parts.append(jnp.where(picked > 0.5, sj, NEG))
            _flash_step(jnp.concatenate(parts, axis=-1), va_ref[keys, :], m_ref.at[rows, :], acc_ref.at[rows, :])

    _far_blocks(jnp.maximum(qi - 1, 0), lambda kb: step(kb, None, False))

    @pl.when(qi >= 1)
    def _():
        step(qi - 1, bias_ref[0], False)
        step(qi, bias_ref[1], True)

    @pl.when(qi == 0)
    def _():
        step(0, bias_ref[1], True)

    acc = acc_ref[...]
    o = acc[:, :w] / acc[:, w:]
    lane = lax.broadcasted_iota(jnp.int32, (t, w), 1)
    o_ref[0] = jnp.where(lane < HD_B, o[:t], o[t:])


def moba_attention_prompt(z, rel_bias):
    b, t_len, _ = z.shape
    t = ATTN_TILE
    assert t_len % t == 0 and t % MOBA_BLOCK == 0 and ATTN_ROW_CHUNK == MOBA_BLOCK
    nb = t_len // MOBA_BLOCK
    assert nb <= LANES
    w = 2 * HD_B
    return pl.pallas_call(
        functools.partial(_moba_attn_kernel, nb=nb),
        grid=(b, H_B // 2, t_len // t),
        in_specs=[pl.BlockSpec(memory_space=pltpu.SMEM),
                  pl.BlockSpec((1, t, w), lambda bi, hp, qi: (bi, qi, COL_QB // w + hp)),
                  pl.BlockSpec((1, t_len, w), lambda bi, hp, qi: (bi, 0, COL_KB // w + hp)),
                  pl.BlockSpec((1, t_len, w), lambda bi, hp, qi: (bi, 0, COL_VB // w + hp))],
        out_specs=pl.BlockSpec((1, t, w), lambda bi, hp, qi: (bi, qi, hp)),
        out_shape=jax.ShapeDtypeStruct((b, t_len, W_B), F32),
        scratch_shapes=[pltpu.VMEM((2, 2 * t, t), F32),
                        pltpu.VMEM((t_len, w), BF16),
                        pltpu.VMEM((t_len, 2 * w), BF16),
                        pltpu.VMEM((LANES, w), F32),
                        pltpu.VMEM((2 * t, LANES), F32),
                        pltpu.VMEM((2 * t, w), BF16),
                        pltpu.VMEM((2 * t, LANES), F32),
                        pltpu.VMEM((2 * t, 2 * w), F32)],
        compiler_params=_params(("arbitrary", "arbitrary", "arbitrary")),
        name="moba_attn_prompt",
    )(rel_bias, z, z, z)


def _masked_query_rows(q, lane_starts, width):
    lane = lax.broadcasted_iota(jnp.int32, q.shape, 1)
    return jnp.concatenate(
        [jnp.where(jnp.logical_and(lane >= s, lane < s + width), q, 0.0) for s in lane_starts], axis=0)


def _group_bias(dist, rel_ref, cols):
    return jnp.concatenate([_rel_bias(dist, rel_ref, col) for col in cols], axis=0)


def _pad_to_page(rows):
    pad = jnp.zeros((PAGE_SIZE - rows.shape[0], rows.shape[1]), rows.dtype)
    return jnp.concatenate([rows, pad], axis=0).astype(BF16)


def _diff_decode_kernel(pt_ref, rel_ref, lamv_ref, g_ref, q_ref, kn_ref, vn_ref, *rest, lam_init):
    del pt_ref
    npg = PAGES_PER_STEP
    kp_refs, vp_refs = rest[:npg], rest[npg:2 * npg]
    o_ref, wq_ref, m_ref, l_ref, acc_ref = rest[2 * npg:]
    c = pl.program_id(1)
    last = pl.num_programs(1) - 1
    r_new = q_ref.shape[1]
    w = 2 * HD_A
    grp = 2 * r_new
    rows = H_A * grp
    flat = PAGE_SIZE * H_A
    cols = [mp * H_A + h for h in range(H_A) for mp in range(2)]

    @pl.when(c == 0)
    def _():
        q = q_ref[0] * (HD_A ** -0.5)
        lane = lax.broadcasted_iota(jnp.int32, (r_new, w), 1)
        parts = []
        for h in range(H_A):
            qh = q[:, h * w:(h + 1) * w]
            parts += [jnp.where(lane < HD_A, qh, 0.0), jnp.where(lane >= HD_A, qh, 0.0)]
        wq_ref[...] = jnp.concatenate(parts, axis=0).astype(BF16)
        m_ref[...] = jnp.full(m_ref.shape, NEG, F32)
        l_ref[...] = jnp.zeros(l_ref.shape, F32)
        acc_ref[...] = jnp.zeros(acc_ref.shape, F32)

    q_head = lax.broadcasted_iota(jnp.int32, (rows, flat), 0) // grp
    k_head = lax.broadcasted_iota(jnp.int32, (rows, flat), 1) % H_A
    head_ok = q_head == k_head

    s_pages = []
    for p in range(npg):
        kf = kp_refs[p][...].reshape(flat, w).astype(BF16)
        s = _dot_nt(wq_ref[...], kf)
        if p == npg - 1:
            r = lax.broadcasted_iota(jnp.int32, (r_new, flat), 0)
            j = lax.broadcasted_iota(jnp.int32, (r_new, flat), 1) // H_A
            dist = jnp.where(c == last, r + PAGE_SIZE - j, FAR_DISTANCE)
            s = s + _group_bias(dist, rel_ref, cols)
        s_pages.append(jnp.where(head_ok, s, NEG))
    m_prev = m_ref[...]
    m_new = m_prev
    for s in s_pages:
        m_new = jnp.maximum(m_new, jnp.max(s, axis=-1, keepdims=True))
    alpha = jnp.exp(m_prev - m_new)
    l_new = alpha * l_ref[...]
    acc = alpha * acc_ref[...]
    for p, s in enumerate(s_pages):
        pexp = jnp.exp(s - m_new)
        l_new = l_new + jnp.sum(pexp, axis=-1, keepdims=True)
        vf = vp_refs[p][...].reshape(flat, w).astype(BF16)
        acc = acc + jnp.dot(pexp.astype(BF16), vf, preferred_element_type=F32)
    m_ref[...] = m_new
    l_ref[...] = l_new
    acc_ref[...] = acc

    @pl.when(c == last)
    def _():
        kn = jnp.concatenate([kn_ref[0, :, h * w:(h + 1) * w] for h in range(H_A)], axis=0)
        vn = jnp.concatenate([vn_ref[0, :, h * w:(h + 1) * w] for h in range(H_A)], axis=0)
        r = lax.broadcasted_iota(jnp.int32, (r_new, PAGE_SIZE), 0)
        j = lax.broadcasted_iota(jnp.int32, (r_new, PAGE_SIZE), 1) % r_new
        s = _dot_nt(wq_ref[...], _pad_to_page(kn)) + _group_bias(r - j, rel_ref, cols)
        qr = lax.broadcasted_iota(jnp.int32, s.shape, 0)
        kc = lax.broadcasted_iota(jnp.int32, s.shape, 1)
        ok = jnp.logical_and(kc // r_new == qr // grp, kc % r_new <= qr % r_new)
        _online_softmax_step(jnp.where(ok, s, NEG), _pad_to_page(vn), m_ref, l_ref, acc_ref)

        o = acc_ref[...] / l_ref[...]
        lv = lamv_ref[...]
        lam = (jnp.exp(jnp.sum(lv[0:1] * lv[1:2], axis=-1, keepdims=True))
               - jnp.exp(jnp.sum(lv[2:3] * lv[3:4], axis=-1, keepdims=True)) + lam_init)
        for h in range(H_A):
            o1 = o[h * grp:h * grp + r_new]
            o2 = o[h * grp + r_new:(h + 1) * grp]
            o_ref[0, :, h * w:(h + 1) * w] = _rms(o1 - lam * o2, g_ref[...]) * (1.0 - lam_init)


def diff_attention_decode(z, cache_k, cache_v, page_table, layer, rel_bias, lamv, subln_g, lam_init):
    n_seq, r_new, _ = z.shape
    n_pages = page_table.shape[1]
    assert n_pages % PAGES_PER_STEP == 0 and r_new == SUBLANES and H_A * r_new <= PAGE_SIZE
    w = 2 * HD_A
    rows = 2 * H_A * r_new
    seq_block = lambda col: pl.BlockSpec((1, r_new, W_A), lambda b, c, pt: (b, 0, col // W_A))
    page_specs = [pl.BlockSpec((None, PAGE_SIZE, None, H_A, w),
                               lambda b, c, pt, p=p: (pt[b, c * PAGES_PER_STEP + p], 0, layer, 0, 0))
                  for p in range(PAGES_PER_STEP)]
    grid_spec = pltpu.PrefetchScalarGridSpec(
        num_scalar_prefetch=1,
        grid=(n_seq, n_pages // PAGES_PER_STEP),
        in_specs=[pl.BlockSpec(memory_space=pltpu.SMEM),
                  pl.BlockSpec((4, HD_A), lambda b, c, pt: (0, 0)),
                  pl.BlockSpec((1, w), lambda b, c, pt: (0, 0)),
                  seq_block(COL_QA), seq_block(COL_KA), seq_block(COL_VA)] + page_specs + page_specs,
        out_specs=pl.BlockSpec((1, r_new, W_A), lambda b, c, pt: (b, 0, 0)),
        scratch_shapes=[pltpu.VMEM((rows, w), BF16),
                        pltpu.VMEM((rows, 1), F32),
                        pltpu.VMEM((rows, 1), F32),
                        pltpu.VMEM((rows, w), F32)])
    return pl.pallas_call(
        functools.partial(_diff_decode_kernel, lam_init=lam_init),
        grid_spec=grid_spec,
        out_shape=jax.ShapeDtypeStruct((n_seq, r_new, W_A), F32),
        compiler_params=_params(("arbitrary", "arbitrary")),
        name="diff_attn_decode",
    )(page_table, rel_bias, lamv, subln_g.reshape(1, w), z, z, z,
      *([cache_k] * PAGES_PER_STEP), *([cache_v] * PAGES_PER_STEP))


def _new_rows_scores(wq, k_new, rel_ref, cols):
    r_new = k_new.shape[0]
    r = lax.broadcasted_iota(jnp.int32, (r_new, PAGE_SIZE), 0)
    j = lax.broadcasted_iota(jnp.int32, (r_new, PAGE_SIZE), 1)
    s = _dot_nt(wq, _pad_to_page(k_new)) + _group_bias(r - j, rel_ref, cols)
    q_row = lax.broadcasted_iota(jnp.int32, s.shape, 0) % r_new
    return jnp.where(lax.broadcasted_iota(jnp.int32, s.shape, 1) <= q_row, s, NEG)


def _moba_decode_kernel(pt_ref, rel_ref, q_ref, kn_ref, vn_ref, *rest):
    del pt_ref
    npg = PAGES_PER_STEP
    kp_refs, vp_refs = rest[:npg], rest[npg:2 * npg]
    o_ref, wq_ref, wqf_ref, kmean_ref, mblk_ref, lblk_ref, accblk_ref = rest[2 * npg:]
    c = pl.program_id(1)
    last = pl.num_programs(1) - 1
    r_new = q_ref.shape[1]
    nb = accblk_ref.shape[0]
    pages_per_block = MOBA_BLOCK // PAGE_SIZE
    blocks_per_step = npg // pages_per_block
    cols = [2 * H_A + h for h in range(H_B)]

    @pl.when(c == 0)
    def _():
        qf = _masked_query_rows(q_ref[0], [h * HD_B for h in range(H_B)], HD_B)
        wqf_ref[...] = qf
        wq_ref[...] = (qf * (HD_B ** -0.5)).astype(BF16)
        kmean_ref[...] = jnp.zeros(kmean_ref.shape, F32)
        mblk_ref[...] = jnp.full(mblk_ref.shape, NEG, F32)
        lblk_ref[...] = jnp.zeros(lblk_ref.shape, F32)

    blk_iota = lax.broadcasted_iota(jnp.int32, mblk_ref.shape, 1)
    mean_lane = lax.broadcasted_iota(jnp.int32, kmean_ref.shape, 1)
    for jb in range(blocks_per_step):
        n = c * blocks_per_step + jb
        pages = range(jb * pages_per_block, (jb + 1) * pages_per_block)
        kts = [kp_refs[p][...].reshape(W_B, PAGE_SIZE) for p in pages]
        s_parts = [jnp.dot(wq_ref[...], kt.astype(BF16), preferred_element_type=F32) for kt in kts]
        if jb == blocks_per_step - 1:
            r = lax.broadcasted_iota(jnp.int32, (r_new, PAGE_SIZE), 0)
            j = lax.broadcasted_iota(jnp.int32, (r_new, PAGE_SIZE), 1)
            dist = jnp.where(c == last, r + PAGE_SIZE - j, FAR_DISTANCE)
            s_parts[-1] = s_parts[-1] + _group_bias(dist, rel_ref, cols)
        s = jnp.concatenate(s_parts, axis=-1)
        m = jnp.max(s, axis=-1, keepdims=True)
        pexp = jnp.exp(s - m)
        acc = sum(_dot_nt(pexp[:, i * PAGE_SIZE:(i + 1) * PAGE_SIZE].astype(BF16),
                          vp_refs[p][...].reshape(W_B, PAGE_SIZE).astype(BF16))
                  for i, p in enumerate(pages))
        accblk_ref[n] = acc
        mblk_ref[...] = jnp.where(blk_iota == n, m, mblk_ref[...])
        lblk_ref[...] = jnp.where(blk_iota == n, jnp.sum(pexp, axis=-1, keepdims=True), lblk_ref[...])
        kmean = sum(jnp.sum(kt, axis=-1, keepdims=True) for kt in kts) * (1.0 / MOBA_BLOCK)
        kmean_ref[...] = jnp.where(mean_lane == n, kmean, kmean_ref[...])

    @pl.when(c == last)
    def _():
        gate = jnp.dot(wqf_ref[...], kmean_ref[...], precision=lax.Precision.HIGHEST,
                       preferred_element_type=F32)
        sel = _top_k_mask(gate, blk_iota < nb, MOBA_TOPK)
        s_own = _new_rows_scores(wq_ref[...], kn_ref[0], rel_ref, cols)
        mblk = jnp.where(sel, mblk_ref[...], NEG)
        m_all = jnp.maximum(jnp.max(mblk, axis=-1, keepdims=True), jnp.max(s_own, axis=-1, keepdims=True))
        wgt = jnp.where(sel, jnp.exp(mblk - m_all), 0.0)
        p_own = jnp.exp(s_own - m_all)
        denom = (jnp.sum(wgt * lblk_ref[...], axis=-1, keepdims=True)
                 + jnp.sum(p_own, axis=-1, keepdims=True))
        acc = jnp.dot(p_own.astype(BF16), _pad_to_page(vn_ref[0]), preferred_element_type=F32)
        for n in range(nb):
            acc = acc + wgt[:, n:n + 1] * accblk_ref[n]
        o = acc / denom
        lane = lax.broadcasted_iota(jnp.int32, (r_new, W_B), 1)
        out = jnp.zeros((r_new, W_B), F32)
        for h in range(H_B):
            in_head = jnp.logical_and(lane >= h * HD_B, lane < (h + 1) * HD_B)
            out = out + jnp.where(in_head, o[h * r_new:(h + 1) * r_new, :], 0.0)
        o_ref[0] = out


def moba_attention_decode(z, cache_kt, cache_vt, page_table, layer, rel_bias):
    n_seq, r_new, _ = z.shape
    n_pages = page_table.shape[1]
    past = n_pages * PAGE_SIZE
    assert n_pages % PAGES_PER_STEP == 0 and past % MOBA_BLOCK == 0 and r_new == SUBLANES
    assert PAGES_PER_STEP % (MOBA_BLOCK // PAGE_SIZE) == 0 and r_new <= MOBA_BLOCK
    nb = past // MOBA_BLOCK
    assert nb <= LANES
    rows = H_B * r_new
    seq_block = lambda col: pl.BlockSpec((1, r_new, W_B), lambda b, c, pt: (b, 0, col // W_B))
    page_specs = [pl.BlockSpec((None, None, H_B, HD_B, PAGE_SIZE),
                               lambda b, c, pt, p=p: (pt[b, c * PAGES_PER_STEP + p], layer, 0, 0, 0))
                  for p in range(PAGES_PER_STEP)]
    grid_spec = pltpu.PrefetchScalarGridSpec(
        num_scalar_prefetch=1,
        grid=(n_seq, n_pages // PAGES_PER_STEP),
        in_specs=[pl.BlockSpec(memory_space=pltpu.SMEM),
                  seq_block(COL_QB), seq_block(COL_KB), seq_block(COL_VB)] + page_specs + page_specs,
        out_specs=pl.BlockSpec((1, r_new, W_B), lambda b, c, pt: (b, 0, 0)),
        scratch_shapes=[pltpu.VMEM((rows, W_B), BF16),
                        pltpu.VMEM((rows, W_B), F32),
                        pltpu.VMEM((W_B, LANES), F32),
                        pltpu.VMEM((rows, LANES), F32),
                        pltpu.VMEM((rows, LANES), F32),
                        pltpu.VMEM((nb, rows, W_B), F32)])
    return pl.pallas_call(
        _moba_decode_kernel,
        grid_spec=grid_spec,
        out_shape=jax.ShapeDtypeStruct((n_seq, r_new, W_B), F32),
        compiler_params=_params(("arbitrary", "arbitrary")),
        name="moba_attn_decode",
    )(page_table, rel_bias, z, z, z, *([cache_kt] * PAGES_PER_STEP), *([cache_vt] * PAGES_PER_STEP))


def _sigmoid(x):
    return 1.0 / (1.0 + jnp.exp(-x))


def _mix_out_kernel(oa_ref, ob_ref, ga_ref, gb_ref, x_ref, wa_ref, wb_ref, wo_ref, o_ref):
    ya = jnp.dot(oa_ref[...].astype(BF16), wa_ref[...], preferred_element_type=F32)
    yb = jnp.dot(ob_ref[...].astype(BF16), wb_ref[...], preferred_element_type=F32)
    mixed = _sigmoid(ga_ref[...]) * ya + _sigmoid(gb_ref[...]) * yb
    o_ref[...] = x_ref[...] + jnp.dot(mixed.astype(BF16), wo_ref[...], preferred_element_type=F32)


def mix_out(oa, ob, z, x, wa, wb, wo):
    n, d = x.shape
    tm = min(MIX_ROW_TILE, n)
    assert n % tm == 0
    full = lambda a: pl.BlockSpec(a.shape, lambda i: (0, 0))
    return pl.pallas_call(
        _mix_out_kernel,
        grid=(n // tm,),
        in_specs=[pl.BlockSpec((tm, W_A), lambda i: (i, 0)),
                  pl.BlockSpec((tm, W_B), lambda i: (i, 0)),
                  pl.BlockSpec((tm, d), lambda i: (i, COL_GA // d)),
                  pl.BlockSpec((tm, d), lambda i: (i, COL_GB // d)),
                  pl.BlockSpec((tm, d), lambda i: (i, 0)),
                  full(wa), full(wb), full(wo)],
        out_specs=pl.BlockSpec((tm, d), lambda i: (i, 0)),
        out_shape=jax.ShapeDtypeStruct((n, d), F32),
        compiler_params=_params(("arbitrary",)),
        name="mix_out",
    )(oa, ob, z, z, x, wa, wb, wo)


def _mem_attn_kernel(x_ref, g_ref, k_ref, v_ref, wq_ref, wo_ref, o_ref):
    x = x_ref[0]
    q = jnp.dot(_rms(x, g_ref[...]).astype(BF16), wq_ref[...], preferred_element_type=F32)
    heads = []
    for h in range(H_M):
        sl = slice(h * HD_M, (h + 1) * HD_M)
        s = _dot_nt((q[:, sl] * (HD_M ** -0.5)).astype(BF16), k_ref[0, :, sl].astype(BF16))
        p = jnp.exp(s - jnp.max(s, axis=-1, keepdims=True))
        oh = jnp.dot(p.astype(BF16), v_ref[0, :, sl].astype(BF16), preferred_element_type=F32)
        heads.append(oh / jnp.sum(p, axis=-1, keepdims=True))
    o = jnp.concatenate(heads, axis=-1)
    o_ref[0] = x + jnp.dot(o.astype(BF16), wo_ref[...], preferred_element_type=F32)


def memory_attention(x, g, mem_k, k_col, mem_v, v_col, wq, wo):
    grp, t_len, d = x.shape
    n_mem = mem_k.shape[1]
    tm = min(MIX_ROW_TILE, t_len)
    assert t_len % tm == 0
    return pl.pallas_call(
        _mem_attn_kernel,
        grid=(grp, t_len // tm),
        in_specs=[pl.BlockSpec((1, tm, d), lambda b, i: (b, i, 0)),
                  pl.BlockSpec((1, d), lambda b, i: (0, 0)),
                  pl.BlockSpec((1, n_mem, W_M), lambda b, i: (b, 0, k_col)),
                  pl.BlockSpec((1, n_mem, W_M), lambda b, i: (b, 0, v_col)),
                  pl.BlockSpec(wq.shape, lambda b, i: (0, 0)),
                  pl.BlockSpec(wo.shape, lambda b, i: (0, 0))],
        out_specs=pl.BlockSpec((1, tm, d), lambda b, i: (b, i, 0)),
        out_shape=jax.ShapeDtypeStruct((grp, t_len, d), F32),
        compiler_params=_params(("arbitrary", "arbitrary")),
        name="memory_attention",
    )(x, g.reshape(1, d), mem_k, mem_v, wq, wo)


def _swiglu_hidden(xn, wg, wu):
    a = jnp.dot(xn, wg, preferred_element_type=F32)
    return a * _sigmoid(a) * jnp.dot(xn, wu, preferred_element_type=F32)


def _dense_ffn_kernel(x_ref, g_ref, wg_ref, wu_ref, wd_ref, o_ref, xn_ref, acc_ref):
    j = pl.program_id(1)

    @pl.when(j == 0)
    def _():
        xn_ref[...] = _rms(x_ref[...], g_ref[...]).astype(BF16)
        acc_ref[...] = jnp.zeros(acc_ref.shape, F32)

    hid = _swiglu_hidden(xn_ref[...], wg_ref[...], wu_ref[...])
    acc_ref[...] += jnp.dot(hid.astype(BF16), wd_ref[...], preferred_element_type=F32)

    @pl.when(j == pl.num_programs(1) - 1)
    def _():
        o_ref[...] = x_ref[...] + acc_ref[...]


def dense_ffn(x, g, wg, wu, wd):
    n, d = x.shape
    f = wg.shape[1]
    tm = min(ROW_TILE, n)
    tf = FF_TILE
    assert n % tm == 0 and f % tf == 0
    return pl.pallas_call(
        _dense_ffn_kernel,
        grid=(n // tm, f // tf),
        in_specs=[pl.BlockSpec((tm, d), lambda i, j: (i, 0)),
                  pl.BlockSpec((1, d), lambda i, j: (0, 0)),
                  pl.BlockSpec((d, tf), lambda i, j: (0, j)),
                  pl.BlockSpec((d, tf), lambda i, j: (0, j)),
                  pl.BlockSpec((tf, d), lambda i, j: (j, 0))],
        out_specs=pl.BlockSpec((tm, d), lambda i, j: (i, 0)),
        out_shape=jax.ShapeDtypeStruct((n, d), F32),
        scratch_shapes=[pltpu.VMEM((tm, d), BF16), pltpu.VMEM((tm, d), F32)],
        compiler_params=_params(("arbitrary", "arbitrary")),
        name="dense_ffn",
    )(x, g.reshape(1, d), wg, wu, wd)


def _moe_ffn_kernel(x_ref, g_ref, r_ref, wg_ref, wu_ref, wd_ref, o_ref,
                    xn_ref, gate_ref, rank_ref, rankt_ref, membt_ref, xc_ref, yc_ref, cnt_ref, *, cap):
    e = pl.program_id(1)
    j = pl.program_id(2)
    last_j = pl.num_programs(2) - 1
    tm = x_ref.shape[0]

    @pl.when(jnp.logical_and(e == 0, j == 0))
    def _():
        xn = _rms(x_ref[...], g_ref[...])
        xn_ref[...] = xn.astype(BF16)
        o_ref[...] = x_ref[...]
        logits = jnp.dot(xn, r_ref[...], precision=lax.Precision.HIGHEST, preferred_element_type=F32)
        lane = lax.broadcasted_iota(jnp.int32, logits.shape, 1).astype(F32)
        logits = jnp.where(lane < N_EXPERTS, logits, -jnp.inf)
        v1 = jnp.max(logits, axis=-1, keepdims=True)
        i1 = jnp.min(jnp.where(logits == v1, lane, float(LANES)), axis=-1, keepdims=True)
        rest = jnp.where(lane == i1, -jnp.inf, logits)
        v2 = jnp.max(rest, axis=-1, keepdims=True)
        i2 = jnp.min(jnp.where(rest == v2, lane, float(LANES)), axis=-1, keepdims=True)
        e2 = jnp.exp(v2 - v1)
        denom = 1.0 + e2
        gate_ref[...] = jnp.where(lane == i1, 1.0 / denom, 0.0) + jnp.where(lane == i2, e2 / denom, 0.0)
        memb = jnp.where(gate_ref[...] > 0.0, 1.0, 0.0)
        r = lax.broadcasted_iota(jnp.int32, (tm, tm), 0)
        c = lax.broadcasted_iota(jnp.int32, (tm, tm), 1)
        rank_ref[...] = jnp.dot(jnp.where(c < r, 1.0, 0.0).astype(BF16), memb.astype(BF16),
                                preferred_element_type=F32)
        membt = memb.T
        membt_ref[...] = membt
        rankt_ref[...] = jnp.dot(membt.astype(BF16), jnp.where(r < c, 1.0, 0.0).astype(BF16),
                                 preferred_element_type=F32)

    def expert_column(ref):
        lane = lax.broadcasted_iota(jnp.int32, ref.shape, 1)
        return jnp.sum(jnp.where(lane == e, ref[...], 0.0), axis=-1, keepdims=True)

    @pl.when(j == 0)
    def _():
        cnt_ref[0] = jnp.sum(jnp.where(expert_column(gate_ref) > 0.0, 1.0, 0.0)).astype(jnp.int32)

    n_chunks = (cnt_ref[0] + (cap - 1)) // cap

    @pl.when(j == 0)
    def _():
        rt = rankt_ref[pl.ds(e, 1), :]
        mt = membt_ref[pl.ds(e, 1), :]

        def gather(ci, carry):
            base = pl.multiple_of(ci * cap, cap)
            slot = (lax.broadcasted_iota(jnp.int32, (cap, tm), 0) + base).astype(F32)
            onehot = jnp.where(jnp.logical_and(slot == rt, mt > 0.5), 1.0, 0.0).astype(BF16)
            xc_ref[pl.ds(base, cap), :] = jnp.dot(onehot, xn_ref[...], preferred_element_type=F32).astype(BF16)
            yc_ref[pl.ds(base, cap), :] = jnp.zeros((cap, yc_ref.shape[1]), F32)
            return carry

        lax.fori_loop(0, n_chunks, gather, 0)

    def ffn(ci, carry):
        rows = pl.ds(pl.multiple_of(ci * cap, cap), cap)
        hid = _swiglu_hidden(xc_ref[rows, :], wg_ref[0], wu_ref[0])
        yc_ref[rows, :] += jnp.dot(hid.astype(BF16), wd_ref[0], preferred_element_type=F32)
        return carry

    lax.fori_loop(0, n_chunks, ffn, 0)

    @pl.when(j == last_j)
    def _():
        rank_e = expert_column(rank_ref)
        gate_e = expert_column(gate_ref)

        def scatter(ci, carry):
            base = pl.multiple_of(ci * cap, cap)
            slot = (lax.broadcasted_iota(jnp.int32, (tm, cap), 1) + base).astype(F32)
            onehot_t = jnp.where(slot == rank_e, gate_e, 0.0).astype(BF16)
            o_ref[...] += jnp.dot(onehot_t, yc_ref[pl.ds(base, cap), :].astype(BF16), preferred_element_type=F32)
            return carry

        lax.fori_loop(0, n_chunks, scatter, 0)


def moe_ffn(x, g, router_padded, wg, wu, wd):
    n, d = x.shape
    n_exp, _, f = wg.shape
    tm = min(ROW_TILE, n)
    tf = MOE_FF_TILE if f % MOE_FF_TILE == 0 else FF_TILE
    cap = min(MOE_CHUNK, tm)
    cap_rows = -(-tm // cap) * cap
    assert n % tm == 0 and f % tf == 0 and n_exp == N_EXPERTS and tm % LANES == 0 and cap % (2 * SUBLANES) == 0
    return pl.pallas_call(
        functools.partial(_moe_ffn_kernel, cap=cap),
        grid=(n // tm, n_exp, f // tf),
        in_specs=[pl.BlockSpec((tm, d), lambda i, e, j: (i, 0)),
                  pl.BlockSpec((1, d), lambda i, e, j: (0, 0)),
                  pl.BlockSpec((d, LANES), lambda i, e, j: (0, 0)),
                  pl.BlockSpec((1, d, tf), lambda i, e, j: (e, 0, j)),
                  pl.BlockSpec((1, d, tf), lambda i, e, j: (e, 0, j)),
                  pl.BlockSpec((1, tf, d), lambda i, e, j: (e, j, 0))],
        out_specs=pl.BlockSpec((tm, d), lambda i, e, j: (i, 0)),
        out_shape=jax.ShapeDtypeStruct((n, d), F32),
        scratch_shapes=[pltpu.VMEM((tm, d), BF16),
                        pltpu.VMEM((tm, LANES), F32),
                        pltpu.VMEM((tm, LANES), F32),
                        pltpu.VMEM((LANES, tm), F32),
                        pltpu.VMEM((LANES, tm), F32),
                        pltpu.VMEM((cap_rows, d), BF16),
                        pltpu.VMEM((cap_rows, d), F32),
                        pltpu.SMEM((1,), jnp.int32)],
        compiler_params=_params(("arbitrary", "arbitrary", "arbitrary")),
        name="moe_ffn",
    )(x, g.reshape(1, d), router_padded, wg, wu, wd)


def kernel(x_prompt, x_sample, mem_prompt, cache_a_k, cache_a_v, cache_b_k, cache_b_v, cache_mem_k, cache_mem_v, page_table, rel_bias, norm_mix, w_in, lambda_q1, lambda_k1, lambda_q2, lambda_k2, subln_gain, w_br_a, w_br_b, w_out, norm_mem_q, norm_mem_kv, w_mem_q, w_mem_kv, w_mem_o, norm_ffn, ffn_w_gate, ffn_w_up, ffn_w_down, moe_router, moe_w_gate, moe_w_up, moe_w_down, norm_final):
    bsz, seq, d = x_prompt.shape
    n_seq, r_new, _ = x_sample.shape
    n_mem = mem_prompt.shape[1]
    depth = w_in.shape[0]

    cb_kt = jnp.transpose(cache_b_k, (0, 2, 3, 4, 1))
    cb_vt = jnp.transpose(cache_b_v, (0, 2, 3, 4, 1))
    cm_k = cache_mem_k.reshape(n_seq, n_mem, depth * W_M)
    cm_v = cache_mem_v.reshape(n_seq, n_mem, depth * W_M)

    hp = x_prompt.reshape(bsz * seq, d)
    hs = x_sample.reshape(n_seq * r_new, d)
    zs_p, zs_s, mkvs = [], [], []
    for l in range(depth):
        lam_init = 0.8 - 0.6 * math.exp(-0.3 * l)
        lamv = jnp.stack([lambda_q1[l], lambda_k1[l], lambda_q2[l], lambda_k2[l]]).astype(F32)
        w_in_l = w_in[l].astype(BF16)
        wa, wb, wo = w_br_a[l].astype(BF16), w_br_b[l].astype(BF16), w_out[l].astype(BF16)

        zp = norm_matmul(hp, norm_mix[l], w_in_l)
        zp3 = zp.reshape(bsz, seq, IN_COLS)
        oa = diff_attention_prompt(zp3, rel_bias, lamv, subln_gain[l], lam_init)
        ob = moba_attention_prompt(zp3, rel_bias)
        hp = mix_out(oa.reshape(bsz * seq, W_A), ob.reshape(bsz * seq, W_B), zp, hp, wa, wb, wo)
        zs_p.append(zp3)

        zs = norm_matmul(hs, norm_mix[l], w_in_l)
        zs3 = zs.reshape(n_seq, r_new, IN_COLS)
        oa_s = diff_attention_decode(zs3, cache_a_k, cache_a_v, page_table, l, rel_bias, lamv, subln_gain[l], lam_init)
        ob_s = moba_attention_decode(zs3, cb_kt, cb_vt, page_table, l, rel_bias)
        hs = mix_out(oa_s.reshape(n_seq * r_new, W_A), ob_s.reshape(n_seq * r_new, W_B), zs, hs, wa, wb, wo)
        zs_s.append(zs3)

        mkv = norm_matmul(mem_prompt.reshape(bsz * n_mem, d), norm_mem_kv[l], w_mem_kv[l].astype(BF16))
        mkv3 = mkv.reshape(bsz, n_mem, 2 * W_M)
        mkvs.append(mkv3)
        wq, wmo = w_mem_q[l].astype(BF16), w_mem_o[l].astype(BF16)
        hp = memory_attention(hp.reshape(bsz, seq, d), norm_mem_q[l], mkv3, 0, mkv3, 1, wq, wmo).reshape(bsz * seq, d)
        hs = memory_attention(hs.reshape(n_seq, r_new, d), norm_mem_q[l], cm_k, l, cm_v, l, wq, wmo).reshape(n_seq * r_new, d)

        i = l // 2
        if l % 2 == 0:
            wg, wu, wd = ffn_w_gate[i].astype(BF16), ffn_w_up[i].astype(BF16), ffn_w_down[i].astype(BF16)
            hp = dense_ffn(hp, norm_ffn[l], wg, wu, wd)
            hs = dense_ffn(hs, norm_ffn[l], wg, wu, wd)
        else:
            wg, wu, wd = moe_w_gate[i].astype(BF16), moe_w_up[i].astype(BF16), moe_w_down[i].astype(BF16)
            router = jnp.pad(moe_router[i].astype(F32), ((0, 0), (0, LANES - N_EXPERTS)))
            hp = moe_ffn(hp, norm_ffn[l], router, wg, wu, wd)
            hs = moe_ffn(hs, norm_ffn[l], router, wg, wu, wd)

    y_prompt = rmsnorm(hp, norm_final).reshape(bsz, seq, d)
    y_sample = rmsnorm(hs, norm_final).reshape(n_seq, r_new, d)

    def stacked(zs, col, width, heads):
        rows = jnp.stack([z[:, :, col:col + width] for z in zs], axis=2)
        return rows.reshape(rows.shape[0], rows.shape[1], depth, heads, width // heads)

    new_mem_k = jnp.stack([m[:, :, :W_M] for m in mkvs], axis=2).reshape(bsz, n_mem, depth, H_M, HD_M)
    new_mem_v = jnp.stack([m[:, :, W_M:] for m in mkvs], axis=2).reshape(bsz, n_mem, depth, H_M, HD_M)
    return (y_prompt, y_sample,
            stacked(zs_p, COL_KA, W_A, H_A), stacked(zs_p, COL_VA, W_A, H_A),
            stacked(zs_p, COL_KB, W_B, H_B), stacked(zs_p, COL_VB, W_B, H_B),
            new_mem_k, new_mem_v,
            stacked(zs_s, COL_KA, W_A, H_A), stacked(zs_s, COL_VA, W_A, H_A),
            stacked(zs_s, COL_KB, W_B, H_B), stacked(zs_s, COL_VB, W_B, H_B))
```

```python
import functools
import math

import numpy as np
import jax
import jax.numpy as jnp
from jax import lax
from jax.experimental import pallas as pl
from jax.experimental.pallas import tpu as pltpu

F32 = jnp.float32
BF16 = jnp.bfloat16

D_MODEL = 1024
DEPTH = 2
PAGE_SIZE = 128
H_A = 4
HD_A = 64
W_A = H_A * 2 * HD_A
H_B = 8
HD_B = 64
W_B = H_B * HD_B
MOBA_BLOCK = 256
MOBA_TOPK = 3
H_M = 4
HD_M = 128
W_M = H_M * HD_M
N_BUCKETS = 32
MAX_DISTANCE = 128
N_EXPERTS = 8
TOP_K_EXPERTS = 2
IN_COLS = 3 * W_A + 3 * W_B + 2 * D_MODEL
EPS = 1e-6
NEG = -1e30
LOG2E = math.log2(math.e)

LANES = 128
SUBLANES = 8
VMEM_LIMIT_BYTES = 56 * 1024 * 1024

ATTN_TILE = 2 * MOBA_BLOCK
ATTN_ROW_CHUNK = MOBA_BLOCK
ATTN_UNROLL = 4
PAGES_PER_STEP = 16
ROW_TILE = 1024
COL_TILE = 1024
FF_TILE = 256
MOE_CHUNK = 320
MOE_FF_TILE = 1408
MIX_ROW_TILE = 512

COL_QA, COL_KA, COL_VA = 0, W_A, 2 * W_A
COL_QB, COL_KB, COL_VB = 3 * W_A, 3 * W_A + W_B, 3 * W_A + 2 * W_B
COL_GA = 3 * W_A + 3 * W_B
COL_GB = COL_GA + D_MODEL


def _t5_bucket_starts():
    n = np.arange(0, 4 * MAX_DISTANCE, dtype=np.int64)
    max_exact = N_BUCKETS // 2
    nf = np.maximum(n, 1).astype(np.float32)
    large = max_exact + (np.log(nf / np.float32(max_exact))
                         / np.float32(math.log(MAX_DISTANCE / max_exact))
                         * np.float32(N_BUCKETS - max_exact)).astype(np.int32)
    bucket = np.where(n < max_exact, n, np.minimum(large, N_BUCKETS - 1))
    assert np.all(np.diff(bucket) >= 0) and bucket[-1] == N_BUCKETS - 1
    starts = [int(np.argmax(bucket >= b)) for b in range(N_BUCKETS)]
    assert starts[-1] <= MAX_DISTANCE
    return starts


BUCKET_START = _t5_bucket_starts()
FAR_DISTANCE = BUCKET_START[-1]


def _params(semantics):
    return pltpu.CompilerParams(dimension_semantics=semantics, vmem_limit_bytes=VMEM_LIMIT_BYTES)


def _rms(x, g):
    return x * lax.rsqrt(jnp.mean(x * x, axis=-1, keepdims=True) + EPS) * g


def _rel_bias(dist, rel_ref, col, scale=1.0):
    last = rel_ref[N_BUCKETS - 1, col]
    out = jnp.zeros(dist.shape, F32)
    for b in range(N_BUCKETS - 2, -1, -1):
        out = jnp.where(dist < BUCKET_START[b + 1], (rel_ref[b, col] - last) * scale, out)
    return out


def _dot_nt(a, b, precision=None):
    return lax.dot_general(a, b, (((1,), (1,)), ((), ())), precision=precision,
                           preferred_element_type=F32)


def _online_softmax_step(s, v, m_ref, l_ref, acc_ref):
    m_prev = m_ref[...]
    m_new = jnp.maximum(m_prev, jnp.max(s, axis=-1, keepdims=True))
    alpha = jnp.exp(m_prev - m_new)
    p = jnp.exp(s - m_new)
    l_ref[...] = alpha * l_ref[...] + jnp.sum(p, axis=-1, keepdims=True)
    acc_ref[...] = alpha * acc_ref[...] + jnp.dot(p.astype(BF16), v, preferred_element_type=F32)
    m_ref[...] = m_new


def _top_k_mask(gate, valid, k, axis=1):
    idx = lax.broadcasted_iota(jnp.int32, gate.shape, axis).astype(F32)
    g = jnp.where(valid, gate, NEG)
    sel = jnp.zeros(gate.shape, F32)
    for _ in range(k):
        mx = jnp.max(g, axis=axis, keepdims=True)
        first = jnp.min(jnp.where(g == mx, idx, float(gate.shape[axis])), axis=axis, keepdims=True)
        pick = idx == first
        sel = jnp.where(pick, 1.0, sel)
        g = jnp.where(pick, -jnp.inf, g)
    return jnp.logical_and(sel > 0.5, valid)


def _norm_matmul_kernel(x_ref, g_ref, w_ref, o_ref, xn_ref):
    @pl.when(pl.program_id(1) == 0)
    def _():
        xn_ref[...] = _rms(x_ref[...], g_ref[...]).astype(BF16)

    o_ref[...] = jnp.dot(xn_ref[...], w_ref[...], preferred_element_type=F32)


def norm_matmul(x, g, w_bf16):
    n, d = x.shape
    c = w_bf16.shape[1]
    tm = min(ROW_TILE, n)
    tn = min(COL_TILE, c)
    assert n % tm == 0 and c % tn == 0
    return pl.pallas_call(
        _norm_matmul_kernel,
        grid=(n // tm, c // tn),
        in_specs=[pl.BlockSpec((tm, d), lambda i, j: (i, 0)),
                  pl.BlockSpec((1, d), lambda i, j: (0, 0)),
                  pl.BlockSpec((d, tn), lambda i, j: (0, j))],
        out_specs=pl.BlockSpec((tm, tn), lambda i, j: (i, j)),
        out_shape=jax.ShapeDtypeStruct((n, c), F32),
        scratch_shapes=[pltpu.VMEM((tm, d), BF16)],
        compiler_params=_params(("arbitrary", "arbitrary")),
        name="norm_matmul",
    )(x, g.reshape(1, d), w_bf16)


def _rmsnorm_kernel(x_ref, g_ref, o_ref):
    o_ref[...] = _rms(x_ref[...], g_ref[...])


def rmsnorm(x, g):
    n, d = x.shape
    tm = min(ROW_TILE, n)
    return pl.pallas_call(
        _rmsnorm_kernel,
        grid=(n // tm,),
        in_specs=[pl.BlockSpec((tm, d), lambda i: (i, 0)), pl.BlockSpec((1, d), lambda i: (0, 0))],
        out_specs=pl.BlockSpec((tm, d), lambda i: (i, 0)),
        out_shape=jax.ShapeDtypeStruct((n, d), F32),
        compiler_params=_params(("arbitrary",)),
        name="final_rmsnorm",
    )(x, g.reshape(1, d))


def _stack_streams(q, half):
    lane = lax.broadcasted_iota(jnp.int32, q.shape, 1)
    return jnp.concatenate([jnp.where(lane < half, q, 0.0), jnp.where(lane >= half, q, 0.0)], axis=0)


def _fill_tile_bias(bias_ref, rel_ref, cols, t):
    r = lax.broadcasted_iota(jnp.int32, (t, t), 0)
    c = lax.broadcasted_iota(jnp.int32, (t, t), 1)
    d_diag = r - c
    for s, col in enumerate(cols):
        bias_ref[0, s * t:(s + 1) * t, :] = _rel_bias(d_diag + t, rel_ref, col, LOG2E)
        bias_ref[1, s * t:(s + 1) * t, :] = jnp.where(d_diag >= 0, _rel_bias(d_diag, rel_ref, col, LOG2E), NEG)


def _prepare_kv(k_ref, v_ref, kb_ref, va_ref, t):
    n_blocks = k_ref.shape[1] // t
    w = v_ref.shape[2]

    def body(n, carry):
        rows = pl.ds(pl.multiple_of(n * t, t), t)
        kb_ref[rows, :] = k_ref[0, rows, :].astype(BF16)
        va_ref[rows, :w] = v_ref[0, rows, :].astype(BF16)
        va_ref[rows, w:] = jnp.ones((t, w), BF16)
        return carry

    lax.fori_loop(0, n_blocks, body, 0)


def _flash_step(s, va, m_ref, acc_ref):
    m_prev = m_ref[...]
    m_new = jnp.maximum(m_prev, jnp.max(s, axis=-1, keepdims=True))
    alpha = jnp.exp2(m_prev - m_new)
    p = jnp.exp2(s - jnp.concatenate([m_new] * (s.shape[1] // LANES), axis=1))
    acc_ref[...] = (jnp.concatenate([alpha] * (acc_ref.shape[1] // LANES), axis=1) * acc_ref[...]
                    + jnp.dot(p.astype(BF16), va, preferred_element_type=F32))
    m_ref[...] = m_new


def _far_blocks(n_far, step):
    def group(g, carry):
        for u in range(ATTN_UNROLL):
            step(ATTN_UNROLL * g + u)
        return carry

    rest = n_far % ATTN_UNROLL
    base = n_far - rest

    def pair(g, carry):
        step(base + 2 * g)
        step(base + 2 * g + 1)
        return carry

    def single(kb, carry):
        step(kb)
        return carry

    lax.fori_loop(0, n_far // ATTN_UNROLL, group, 0)
    lax.fori_loop(0, rest // 2, pair, 0)
    lax.fori_loop(base + 2 * (rest // 2), n_far, single, 0)


def _diff_attn_kernel(rel_ref, lamv_ref, g_ref, q_ref, k_ref, v_ref, o_ref,
                      bias_ref, kb_ref, va_ref, qs_ref, m_ref, acc_ref, *, lam_init):
    t = ATTN_TILE
    w = 2 * HD_A
    h = pl.program_id(1)
    qi = pl.program_id(2)

    @pl.when(qi == 0)
    def _():
        _fill_tile_bias(bias_ref, rel_ref, (h, H_A + h), t)
        _prepare_kv(k_ref, v_ref, kb_ref, va_ref, t)

    qs_ref[...] = _stack_streams(q_ref[0] * (HD_A ** -0.5 * LOG2E), HD_A).astype(BF16)
    m_ref[...] = jnp.full(m_ref.shape, NEG, F32)
    acc_ref[...] = jnp.zeros(acc_ref.shape, F32)

    def step(kb, bias):
        keys = pl.ds(pl.multiple_of(kb * t, t), t)
        for c in range(2 * t // ATTN_ROW_CHUNK):
            rows = slice(c * ATTN_ROW_CHUNK, (c + 1) * ATTN_ROW_CHUNK)
            s = _dot_nt(qs_ref[rows, :], kb_ref[keys, :])
            if bias is not None:
                s = s + bias[rows, :]
            _flash_step(s, va_ref[keys, :], m_ref.at[rows, :], acc_ref.at[rows, :])

    _far_blocks(jnp.maximum(qi - 1, 0), lambda kb: step(kb, None))

    @pl.when(qi >= 1)
    def _():
        step(qi - 1, bias_ref[0])
        step(qi, bias_ref[1])

    @pl.when(qi == 0)
    def _():
        step(0, bias_ref[1])

    acc = acc_ref[...]
    o = acc[:, :w] / acc[:, w:]
    lv = lamv_ref[...]
    lam = (jnp.exp(jnp.sum(lv[0:1] * lv[1:2], axis=-1, keepdims=True))
           - jnp.exp(jnp.sum(lv[2:3] * lv[3:4], axis=-1, keepdims=True)) + lam_init)
    d = o[:t] - lam * o[t:]
    o_ref[0] = _rms(d, g_ref[...]) * (1.0 - lam_init)


def diff_attention_prompt(z, rel_bias, lamv, subln_g, lam_init):
    b, t_len, _ = z.shape
    t = ATTN_TILE
    assert t_len % t == 0
    w = 2 * HD_A
    return pl.pallas_call(
        functools.partial(_diff_attn_kernel, lam_init=lam_init),
        grid=(b, H_A, t_len // t),
        in_specs=[pl.BlockSpec(memory_space=pltpu.SMEM),
                  pl.BlockSpec((4, HD_A), lambda bi, h, qi: (0, 0)),
                  pl.BlockSpec((1, w), lambda bi, h, qi: (0, 0)),
                  pl.BlockSpec((1, t, w), lambda bi, h, qi: (bi, qi, COL_QA // w + h)),
                  pl.BlockSpec((1, t_len, w), lambda bi, h, qi: (bi, 0, COL_KA // w + h)),
                  pl.BlockSpec((1, t_len, w), lambda bi, h, qi: (bi, 0, COL_VA // w + h))],
        out_specs=pl.BlockSpec((1, t, w), lambda bi, h, qi: (bi, qi, h)),
        out_shape=jax.ShapeDtypeStruct((b, t_len, W_A), F32),
        scratch_shapes=[pltpu.VMEM((2, 2 * t, t), F32),
                        pltpu.VMEM((t_len, w), BF16),
                        pltpu.VMEM((t_len, 2 * w), BF16),
                        pltpu.VMEM((2 * t, w), BF16),
                        pltpu.VMEM((2 * t, LANES), F32),
                        pltpu.VMEM((2 * t, 2 * w), F32)],
        compiler_params=_params(("arbitrary", "arbitrary", "arbitrary")),
        name="diff_attn_prompt",
    )(rel_bias, lamv, subln_g.reshape(1, w), z, z, z)


def _moba_attn_kernel(rel_ref, q_ref, k_ref, v_ref, o_ref,
                      bias_ref, kb_ref, va_ref, kmean_ref, sel_ref, qs_ref, m_ref, acc_ref, *, nb):
    t = ATTN_TILE
    blk_per_tile = t // MOBA_BLOCK
    w = 2 * HD_B
    hp = pl.program_id(1)
    qi = pl.program_id(2)

    @pl.when(qi == 0)
    def _():
        _fill_tile_bias(bias_ref, rel_ref, (2 * H_A + 2 * hp, 2 * H_A + 2 * hp + 1), t)
        _prepare_kv(k_ref, v_ref, kb_ref, va_ref, t)
        kmean_ref[...] = jnp.zeros(kmean_ref.shape, F32)

        def mean_body(n, carry):
            start = pl.multiple_of(n * MOBA_BLOCK, MOBA_BLOCK)
            kmean_ref[pl.ds(n, 1), :] = jnp.mean(k_ref[0, pl.ds(start, MOBA_BLOCK), :], axis=0, keepdims=True)
            return carry

        lax.fori_loop(0, nb, mean_body, 0)

    qf = _stack_streams(q_ref[0], HD_B)
    qs_ref[...] = (qf * (HD_B ** -0.5 * LOG2E)).astype(BF16)
    gate_t = _dot_nt(kmean_ref[...], qf, precision=lax.Precision.HIGHEST)
    blk = lax.broadcasted_iota(jnp.int32, gate_t.shape, 0)
    row_in_tile = lax.broadcasted_iota(jnp.int32, gate_t.shape, 1) % t
    own = qi * blk_per_tile + row_in_tile // MOBA_BLOCK
    sel_ref[...] = jnp.where(_top_k_mask(gate_t, blk < own, MOBA_TOPK, axis=0), 1.0, 0.0).T
    m_ref[...] = jnp.full(m_ref.shape, NEG, F32)
    acc_ref[...] = jnp.zeros(acc_ref.shape, F32)

    def step(kb, bias, diagonal):
        keys = pl.ds(pl.multiple_of(kb * t, t), t)
        for c in range(2 * t // ATTN_ROW_CHUNK):
            rows = slice(c * ATTN_ROW_CHUNK, (c + 1) * ATTN_ROW_CHUNK)
            own_j = (c * ATTN_ROW_CHUNK % t) // MOBA_BLOCK
            s = _dot_nt(qs_ref[rows, :], kb_ref[keys, :])
            if bias is not None:
                s = s + bias[rows, :]
            sel = sel_ref[rows, :]
            lane = lax.broadcasted_iota(jnp.int32, sel.shape, 1)
            parts = []
            for j in range(blk_per_tile):
                sj = s[:, j * MOBA_BLOCK:(j + 1) * MOBA_BLOCK]
                if diagonal and j == own_j:
                    parts.append(sj)
                else:
                    picked = jnp.sum(jnp.where(lane == kb * blk_per_tile + j, sel, 0.0), axis=-1, keepdims=True)
                    parts.append(jnp.where(picked > 0.5, sj, NEG))
            _flash_step(jnp.concatenate(parts, axis=-1), va_ref[keys, :], m_ref.at[rows, :], acc_ref.at[rows, :])

    _far_blocks(jnp.maximum(qi - 1, 0), lambda kb: step(kb, None, False))

    @pl.when(qi >= 1)
    def _():
        step(qi - 1, bias_ref[0], False)
        step(qi, bias_ref[1], True)

    @pl.when(qi == 0)
    def _():
        step(0, bias_ref[1], True)

    acc = acc_ref[...]
    o = acc[:, :w] / acc[:, w:]
    lane = lax.broadcasted_iota(jnp.int32, (t, w), 1)
    o_ref[0] = jnp.where(lane < HD_B, o[:t], o[t:])


def moba_attention_prompt(z, rel_bias):
    b, t_len, _ = z.shape
    t = ATTN_TILE
    assert t_len % t == 0 and t % MOBA_BLOCK == 0 and ATTN_ROW_CHUNK == MOBA_BLOCK
    nb = t_len // MOBA_BLOCK
    assert nb <= LANES
    w = 2 * HD_B
    return pl.pallas_call(
        functools.partial(_moba_attn_kernel, nb=nb),
        grid=(b, H_B // 2, t_len // t),
        in_specs=[pl.BlockSpec(memory_space=pltpu.SMEM),
                  pl.BlockSpec((1, t, w), lambda bi, hp, qi: (bi, qi, COL_QB // w + hp)),
                  pl.BlockSpec((1, t_len, w), lambda bi, hp, qi: (bi, 0, COL_KB // w + hp)),
                  pl.BlockSpec((1, t_len, w), lambda bi, hp, qi: (bi, 0, COL_VB // w + hp))],
        out_specs=pl.BlockSpec((1, t, w), lambda bi, hp, qi: (bi, qi, hp)),
        out_shape=jax.ShapeDtypeStruct((b, t_len, W_B), F32),
        scratch_shapes=[pltpu.VMEM((2, 2 * t, t), F32),
                        pltpu.VMEM((t_len, w), BF16),
                        pltpu.VMEM((t_len, 2 * w), BF16),
                        pltpu.VMEM((LANES, w), F32),
                        pltpu.VMEM((2 * t, LANES), F32),
                        pltpu.VMEM((2 * t, w), BF16),
                        pltpu.VMEM((2 * t, LANES), F32),
                        pltpu.VMEM((2 * t, 2 * w), F32)],
        compiler_params=_params(("arbitrary", "arbitrary", "arbitrary")),
        name="moba_attn_prompt",
    )(rel_bias, z, z, z)


def _masked_query_rows(q, lane_starts, width):
    lane = lax.broadcasted_iota(jnp.int32, q.shape, 1)
    return jnp.concatenate(
        [jnp.where(jnp.logical_and(lane >= s, lane < s + width), q, 0.0) for s in lane_starts], axis=0)


def _group_bias(dist, rel_ref, cols):
    return jnp.concatenate([_rel_bias(dist, rel_ref, col) for col in cols], axis=0)


def _pad_to_page(rows):
    pad = jnp.zeros((PAGE_SIZE - rows.shape[0], rows.shape[1]), rows.dtype)
    return jnp.concatenate([rows, pad], axis=0).astype(BF16)


def _diff_decode_kernel(pt_ref, rel_ref, lamv_ref, g_ref, q_ref, kn_ref, vn_ref, *rest, lam_init):
    del pt_ref
    npg = PAGES_PER_STEP
    kp_refs, vp_refs = rest[:npg], rest[npg:2 * npg]
    o_ref, wq_ref, m_ref, l_ref, acc_ref = rest[2 * npg:]
    c = pl.program_id(1)
    last = pl.num_programs(1) - 1
    r_new = q_ref.shape[1]
    w = 2 * HD_A
    grp = 2 * r_new
    rows = H_A * grp
    flat = PAGE_SIZE * H_A
    cols = [mp * H_A + h for h in range(H_A) for mp in range(2)]

    @pl.when(c == 0)
    def _():
        q = q_ref[0] * (HD_A ** -0.5)
        lane = lax.broadcasted_iota(jnp.int32, (r_new, w), 1)
        parts = []
        for h in range(H_A):
            qh = q[:, h * w:(h + 1) * w]
            parts += [jnp.where(lane < HD_A, qh, 0.0), jnp.where(lane >= HD_A, qh, 0.0)]
        wq_ref[...] = jnp.concatenate(parts, axis=0).astype(BF16)
        m_ref[...] = jnp.full(m_ref.shape, NEG, F32)
        l_ref[...] = jnp.zeros(l_ref.shape, F32)
        acc_ref[...] = jnp.zeros(acc_ref.shape, F32)

    q_head = lax.broadcasted_iota(jnp.int32, (rows, flat), 0) // grp
    k_head = lax.broadcasted_iota(jnp.int32, (rows, flat), 1) % H_A
    head_ok = q_head == k_head

    s_pages = []
    for p in range(npg):
        kf = kp_refs[p][...].reshape(flat, w).astype(BF16)
        s = _dot_nt(wq_ref[...], kf)
        if p == npg - 1:
            r = lax.broadcasted_iota(jnp.int32, (r_new, flat), 0)
            j = lax.broadcasted_iota(jnp.int32, (r_new, flat), 1) // H_A
            dist = jnp.where(c == last, r + PAGE_SIZE - j, FAR_DISTANCE)
            s = s + _group_bias(dist, rel_ref, cols)
        s_pages.append(jnp.where(head_ok, s, NEG))
    m_prev = m_ref[...]
    m_new = m_prev
    for s in s_pages:
        m_new = jnp.maximum(m_new, jnp.max(s, axis=-1, keepdims=True))
    alpha = jnp.exp(m_prev - m_new)
    l_new = alpha * l_ref[...]
    acc = alpha * acc_ref[...]
    for p, s in enumerate(s_pages):
        pexp = jnp.exp(s - m_new)
        l_new = l_new + jnp.sum(pexp, axis=-1, keepdims=True)
        vf = vp_refs[p][...].reshape(flat, w).astype(BF16)
        acc = acc + jnp.dot(pexp.astype(BF16), vf, preferred_element_type=F32)
    m_ref[...] = m_new
    l_ref[...] = l_new
    acc_ref[...] = acc

    @pl.when(c == last)
    def _():
        kn = jnp.concatenate([kn_ref[0, :, h * w:(h + 1) * w] for h in range(H_A)], axis=0)
        vn = jnp.concatenate([vn_ref[0, :, h * w:(h + 1) * w] for h in range(H_A)], axis=0)
        r = lax.broadcasted_iota(jnp.int32, (r_new, PAGE_SIZE), 0)
        j = lax.broadcasted_iota(jnp.int32, (r_new, PAGE_SIZE), 1) % r_new
        s = _dot_nt(wq_ref[...], _pad_to_page(kn)) + _group_bias(r - j, rel_ref, cols)
        qr = lax.broadcasted_iota(jnp.int32, s.shape, 0)
        kc = lax.broadcasted_iota(jnp.int32, s.shape, 1)
        ok = jnp.logical_and(kc // r_new == qr // grp, kc % r_new <= qr % r_new)
        _online_softmax_step(jnp.where(ok, s, NEG), _pad_to_page(vn), m_ref, l_ref, acc_ref)

        o = acc_ref[...] / l_ref[...]
        lv = lamv_ref[...]
        lam = (jnp.exp(jnp.sum(lv[0:1] * lv[1:2], axis=-1, keepdims=True))
               - jnp.exp(jnp.sum(lv[2:3] * lv[3:4], axis=-1, keepdims=True)) + lam_init)
        for h in range(H_A):
            o1 = o[h * grp:h * grp + r_new]
            o2 = o[h * grp + r_new:(h + 1) * grp]
            o_ref[0, :, h * w:(h + 1) * w] = _rms(o1 - lam * o2, g_ref[...]) * (1.0 - lam_init)


def diff_attention_decode(z, cache_k, cache_v, page_table, layer, rel_bias, lamv, subln_g, lam_init):
    n_seq, r_new, _ = z.shape
    n_pages = page_table.shape[1]
    assert n_pages % PAGES_PER_STEP == 0 and r_new == SUBLANES and H_A * r_new <= PAGE_SIZE
    w = 2 * HD_A
    rows = 2 * H_A * r_new
    seq_block = lambda col: pl.BlockSpec((1, r_new, W_A), lambda b, c, pt: (b, 0, col // W_A))
    page_specs = [pl.BlockSpec((None, PAGE_SIZE, None, H_A, w),
                               lambda b, c, pt, p=p: (pt[b, c * PAGES_PER_STEP + p], 0, layer, 0, 0))
                  for p in range(PAGES_PER_STEP)]
    grid_spec = pltpu.PrefetchScalarGridSpec(
        num_scalar_prefetch=1,
        grid=(n_seq, n_pages // PAGES_PER_STEP),
        in_specs=[pl.BlockSpec(memory_space=pltpu.SMEM),
                  pl.BlockSpec((4, HD_A), lambda b, c, pt: (0, 0)),
                  pl.BlockSpec((1, w), lambda b, c, pt: (0, 0)),
                  seq_block(COL_QA), seq_block(COL_KA), seq_block(COL_VA)] + page_specs + page_specs,
        out_specs=pl.BlockSpec((1, r_new, W_A), lambda b, c, pt: (b, 0, 0)),
        scratch_shapes=[pltpu.VMEM((rows, w), BF16),
                        pltpu.VMEM((rows, 1), F32),
                        pltpu.VMEM((rows, 1), F32),
                        pltpu.VMEM((rows, w), F32)])
    return pl.pallas_call(
        functools.partial(_diff_decode_kernel, lam_init=lam_init),
        grid_spec=grid_spec,
        out_shape=jax.ShapeDtypeStruct((n_seq, r_new, W_A), F32),
        compiler_params=_params(("arbitrary", "arbitrary")),
        name="diff_attn_decode",
    )(page_table, rel_bias, lamv, subln_g.reshape(1, w), z, z, z,
      *([cache_k] * PAGES_PER_STEP), *([cache_v] * PAGES_PER_STEP))


def _new_rows_scores(wq, k_new, rel_ref, cols):
    r_new = k_new.shape[0]
    r = lax.broadcasted_iota(jnp.int32, (r_new, PAGE_SIZE), 0)
    j = lax.broadcasted_iota(jnp.int32, (r_new, PAGE_SIZE), 1)
    s = _dot_nt(wq, _pad_to_page(k_new)) + _group_bias(r - j, rel_ref, cols)
    q_row = lax.broadcasted_iota(jnp.int32, s.shape, 0) % r_new
    return jnp.where(lax.broadcasted_iota(jnp.int32, s.shape, 1) <= q_row, s, NEG)


def _moba_decode_kernel(pt_ref, rel_ref, q_ref, kn_ref, vn_ref, *rest):
    del pt_ref
    npg = PAGES_PER_STEP
    kp_refs, vp_refs = rest[:npg], rest[npg:2 * npg]
    o_ref, wq_ref, wqf_ref, kmean_ref, mblk_ref, lblk_ref, accblk_ref = rest[2 * npg:]
    c = pl.program_id(1)
    last = pl.num_programs(1) - 1
    r_new = q_ref.shape[1]
    nb = accblk_ref.shape[0]
    pages_per_block = MOBA_BLOCK // PAGE_SIZE
    blocks_per_step = npg // pages_per_block
    cols = [2 * H_A + h for h in range(H_B)]

    @pl.when(c == 0)
    def _():
        qf = _masked_query_rows(q_ref[0], [h * HD_B for h in range(H_B)], HD_B)
        wqf_ref[...] = qf
        wq_ref[...] = (qf * (HD_B ** -0.5)).astype(BF16)
        kmean_ref[...] = jnp.zeros(kmean_ref.shape, F32)
        mblk_ref[...] = jnp.full(mblk_ref.shape, NEG, F32)
        lblk_ref[...] = jnp.zeros(lblk_ref.shape, F32)

    blk_iota = lax.broadcasted_iota(jnp.int32, mblk_ref.shape, 1)
    mean_lane = lax.broadcasted_iota(jnp.int32, kmean_ref.shape, 1)
    for jb in range(blocks_per_step):
        n = c * blocks_per_step + jb
        pages = range(jb * pages_per_block, (jb + 1) * pages_per_block)
        kts = [kp_refs[p][...].reshape(W_B, PAGE_SIZE) for p in pages]
        s_parts = [jnp.dot(wq_ref[...], kt.astype(BF16), preferred_element_type=F32) for kt in kts]
        if jb == blocks_per_step - 1:
            r = lax.broadcasted_iota(jnp.int32, (r_new, PAGE_SIZE), 0)
            j = lax.broadcasted_iota(jnp.int32, (r_new, PAGE_SIZE), 1)
            dist = jnp.where(c == last, r + PAGE_SIZE - j, FAR_DISTANCE)
            s_parts[-1] = s_parts[-1] + _group_bias(dist, rel_ref, cols)
        s = jnp.concatenate(s_parts, axis=-1)
        m = jnp.max(s, axis=-1, keepdims=True)
        pexp = jnp.exp(s - m)
        acc = sum(_dot_nt(pexp[:, i * PAGE_SIZE:(i + 1) * PAGE_SIZE].astype(BF16),
                          vp_refs[p][...].reshape(W_B, PAGE_SIZE).astype(BF16))
                  for i, p in enumerate(pages))
        accblk_ref[n] = acc
        mblk_ref[...] = jnp.where(blk_iota == n, m, mblk_ref[...])
        lblk_ref[...] = jnp.where(blk_iota == n, jnp.sum(pexp, axis=-1, keepdims=True), lblk_ref[...])
        kmean = sum(jnp.sum(kt, axis=-1, keepdims=True) for kt in kts) * (1.0 / MOBA_BLOCK)
        kmean_ref[...] = jnp.where(mean_lane == n, kmean, kmean_ref[...])

    @pl.when(c == last)
    def _():
        gate = jnp.dot(wqf_ref[...], kmean_ref[...], precision=lax.Precision.HIGHEST,
                       preferred_element_type=F32)
        sel = _top_k_mask(gate, blk_iota < nb, MOBA_TOPK)
        s_own = _new_rows_scores(wq_ref[...], kn_ref[0], rel_ref, cols)
        mblk = jnp.where(sel, mblk_ref[...], NEG)
        m_all = jnp.maximum(jnp.max(mblk, axis=-1, keepdims=True), jnp.max(s_own, axis=-1, keepdims=True))
        wgt = jnp.where(sel, jnp.exp(mblk - m_all), 0.0)
        p_own = jnp.exp(s_own - m_all)
        denom = (jnp.sum(wgt * lblk_ref[...], axis=-1, keepdims=True)
                 + jnp.sum(p_own, axis=-1, keepdims=True))
        acc = jnp.dot(p_own.astype(BF16), _pad_to_page(vn_ref[0]), preferred_element_type=F32)
        for n in range(nb):
            acc = acc + wgt[:, n:n + 1] * accblk_ref[n]
        o = acc / denom
        lane = lax.broadcasted_iota(jnp.int32, (r_new, W_B), 1)
        out = jnp.zeros((r_new, W_B), F32)
        for h in range(H_B):
            in_head = jnp.logical_and(lane >= h * HD_B, lane < (h + 1) * HD_B)
            out = out + jnp.where(in_head, o[h * r_new:(h + 1) * r_new, :], 0.0)
        o_ref[0] = out


def moba_attention_decode(z, cache_kt, cache_vt, page_table, layer, rel_bias):
    n_seq, r_new, _ = z.shape
    n_pages = page_table.shape[1]
    past = n_pages * PAGE_SIZE
    assert n_pages % PAGES_PER_STEP == 0 and past % MOBA_BLOCK == 0 and r_new == SUBLANES
    assert PAGES_PER_STEP % (MOBA_BLOCK // PAGE_SIZE) == 0 and r_new <= MOBA_BLOCK
    nb = past // MOBA_BLOCK
    assert nb <= LANES
    rows = H_B * r_new
    seq_block = lambda col: pl.BlockSpec((1, r_new, W_B), lambda b, c, pt: (b, 0, col // W_B))
    page_specs = [pl.BlockSpec((None, None, H_B, HD_B, PAGE_SIZE),
                               lambda b, c, pt, p=p: (pt[b, c * PAGES_PER_STEP + p], layer, 0, 0, 0))
                  for p in range(PAGES_PER_STEP)]
    grid_spec = pltpu.PrefetchScalarGridSpec(
        num_scalar_prefetch=1,
        grid=(n_seq, n_pages // PAGES_PER_STEP),
        in_specs=[pl.BlockSpec(memory_space=pltpu.SMEM),
                  seq_block(COL_QB), seq_block(COL_KB), seq_block(COL_VB)] + page_specs + page_specs,
        out_specs=pl.BlockSpec((1, r_new, W_B), lambda b, c, pt: (b, 0, 0)),
        scratch_shapes=[pltpu.VMEM((rows, W_B), BF16),
                        pltpu.VMEM((rows, W_B), F32),
                        pltpu.VMEM((W_B, LANES), F32),
                        pltpu.VMEM((rows, LANES), F32),
                        pltpu.VMEM((rows, LANES), F32),
                        pltpu.VMEM((nb, rows, W_B), F32)])
    return pl.pallas_call(
        _moba_decode_kernel,
        grid_spec=grid_spec,
        out_shape=jax.ShapeDtypeStruct((n_seq, r_new, W_B), F32),
        compiler_params=_params(("arbitrary", "arbitrary")),
        name="moba_attn_decode",
    )(page_table, rel_bias, z, z, z, *([cache_kt] * PAGES_PER_STEP), *([cache_vt] * PAGES_PER_STEP))


def _sigmoid(x):
    return 1.0 / (1.0 + jnp.exp(-x))


def _mix_out_kernel(oa_ref, ob_ref, ga_ref, gb_ref, x_ref, wa_ref, wb_ref, wo_ref, o_ref):
    ya = jnp.dot(oa_ref[...].astype(BF16), wa_ref[...], preferred_element_type=F32)
    yb = jnp.dot(ob_ref[...].astype(BF16), wb_ref[...], preferred_element_type=F32)
    mixed = _sigmoid(ga_ref[...]) * ya + _sigmoid(gb_ref[...]) * yb
    o_ref[...] = x_ref[...] + jnp.dot(mixed.astype(BF16), wo_ref[...], preferred_element_type=F32)


def mix_out(oa, ob, z, x, wa, wb, wo):
    n, d = x.shape
    tm = min(MIX_ROW_TILE, n)
    assert n % tm == 0
    full = lambda a: pl.BlockSpec(a.shape, lambda i: (0, 0))
    return pl.pallas_call(
        _mix_out_kernel,
        grid=(n // tm,),
        in_specs=[pl.BlockSpec((tm, W_A), lambda i: (i, 0)),
                  pl.BlockSpec((tm, W_B), lambda i: (i, 0)),
                  pl.BlockSpec((tm, d), lambda i: (i, COL_GA // d)),
                  pl.BlockSpec((tm, d), lambda i: (i, COL_GB // d)),
                  pl.BlockSpec((tm, d), lambda i: (i, 0)),
                  full(wa), full(wb), full(wo)],
        out_specs=pl.BlockSpec((tm, d), lambda i: (i, 0)),
        out_shape=jax.ShapeDtypeStruct((n, d), F32),
        compiler_params=_params(("arbitrary",)),
        name="mix_out",
    )(oa, ob, z, z, x, wa, wb, wo)


def _mem_attn_kernel(x_ref, g_ref, k_ref, v_ref, wq_ref, wo_ref, o_ref):
    x = x_ref[0]
    q = jnp.dot(_rms(x, g_ref[...]).astype(BF16), wq_ref[...], preferred_element_type=F32)
    heads = []
    for h in range(H_M):
        sl = slice(h * HD_M, (h + 1) * HD_M)
        s = _dot_nt((q[:, sl] * (HD_M ** -0.5)).astype(BF16), k_ref[0, :, sl].astype(BF16))
        p = jnp.exp(s - jnp.max(s, axis=-1, keepdims=True))
        oh = jnp.dot(p.astype(BF16), v_ref[0, :, sl].astype(BF16), preferred_element_type=F32)
        heads.append(oh / jnp.sum(p, axis=-1, keepdims=True))
    o = jnp.concatenate(heads, axis=-1)
    o_ref[0] = x + jnp.dot(o.astype(BF16), wo_ref[...], preferred_element_type=F32)


def memory_attention(x, g, mem_k, k_col, mem_v, v_col, wq, wo):
    grp, t_len, d = x.shape
    n_mem = mem_k.shape[1]
    tm = min(MIX_ROW_TILE, t_len)
    assert t_len % tm == 0
    return pl.pallas_call(
        _mem_attn_kernel,
        grid=(grp, t_len // tm),
        in_specs=[pl.BlockSpec((1, tm, d), lambda b, i: (b, i, 0)),
                  pl.BlockSpec((1, d), lambda b, i: (0, 0)),
                  pl.BlockSpec((1, n_mem, W_M), lambda b, i: (b, 0, k_col)),
                  pl.BlockSpec((1, n_mem, W_M), lambda b, i: (b, 0, v_col)),
                  pl.BlockSpec(wq.shape, lambda b, i: (0, 0)),
                  pl.BlockSpec(wo.shape, lambda b, i: (0, 0))],
        out_specs=pl.BlockSpec((1, tm, d), lambda b, i: (b, i, 0)),
        out_shape=jax.ShapeDtypeStruct((grp, t_len, d), F32),
        compiler_params=_params(("arbitrary", "arbitrary")),
        name="memory_attention",
    )(x, g.reshape(1, d), mem_k, mem_v, wq, wo)


def _swiglu_hidden(xn, wg, wu):
    a = jnp.dot(xn, wg, preferred_element_type=F32)
    return a * _sigmoid(a) * jnp.dot(xn, wu, preferred_element_type=F32)


def _dense_ffn_kernel(x_ref, g_ref, wg_ref, wu_ref, wd_ref, o_ref, xn_ref, acc_ref):
    j = pl.program_id(1)

    @pl.when(j == 0)
    def _():
        xn_ref[...] = _rms(x_ref[...], g_ref[...]).astype(BF16)
        acc_ref[...] = jnp.zeros(acc_ref.shape, F32)

    hid = _swiglu_hidden(xn_ref[...], wg_ref[...], wu_ref[...])
    acc_ref[...] += jnp.dot(hid.astype(BF16), wd_ref[...], preferred_element_type=F32)

    @pl.when(j == pl.num_programs(1) - 1)
    def _():
        o_ref[...] = x_ref[...] + acc_ref[...]


def dense_ffn(x, g, wg, wu, wd):
    n, d = x.shape
    f = wg.shape[1]
    tm = min(ROW_TILE, n)
    tf = FF_TILE
    assert n % tm == 0 and f % tf == 0
    return pl.pallas_call(
        _dense_ffn_kernel,
        grid=(n // tm, f // tf),
        in_specs=[pl.BlockSpec((tm, d), lambda i, j: (i, 0)),
                  pl.BlockSpec((1, d), lambda i, j: (0, 0)),
                  pl.BlockSpec((d, tf), lambda i, j: (0, j)),
                  pl.BlockSpec((d, tf), lambda i, j: (0, j)),
                  pl.BlockSpec((tf, d), lambda i, j: (j, 0))],
        out_specs=pl.BlockSpec((tm, d), lambda i, j: (i, 0)),
        out_shape=jax.ShapeDtypeStruct((n, d), F32),
        scratch_shapes=[pltpu.VMEM((tm, d), BF16), pltpu.VMEM((tm, d), F32)],
        compiler_params=_params(("arbitrary", "arbitrary")),
        name="dense_ffn",
    )(x, g.reshape(1, d), wg, wu, wd)


def _moe_ffn_kernel(x_ref, g_ref, r_ref, wg_ref, wu_ref, wd_ref, o_ref,
                    xn_ref, gate_ref, rank_ref, rankt_ref, membt_ref, xc_ref, yc_ref, cnt_ref, *, cap):
    e = pl.program_id(1)
    j = pl.program_id(2)
    last_j = pl.num_programs(2) - 1
    tm = x_ref.shape[0]

    @pl.when(jnp.logical_and(e == 0, j == 0))
    def _():
        xn = _rms(x_ref[...], g_ref[...])
        xn_ref[...] = xn.astype(BF16)
        o_ref[...] = x_ref[...]
        logits = jnp.dot(xn, r_ref[...], precision=lax.Precision.HIGHEST, preferred_element_type=F32)
        lane = lax.broadcasted_iota(jnp.int32, logits.shape, 1).astype(F32)
        logits = jnp.where(lane < N_EXPERTS, logits, -jnp.inf)
        v1 = jnp.max(logits, axis=-1, keepdims=True)
        i1 = jnp.min(jnp.where(logits == v1, lane, float(LANES)), axis=-1, keepdims=True)
        rest = jnp.where(lane == i1, -jnp.inf, logits)
        v2 = jnp.max(rest, axis=-1, keepdims=True)
        i2 = jnp.min(jnp.where(rest == v2, lane, float(LANES)), axis=-1, keepdims=True)
        e2 = jnp.exp(v2 - v1)
        denom = 1.0 + e2
        gate_ref[...] = jnp.where(lane == i1, 1.0 / denom, 0.0) + jnp.where(lane == i2, e2 / denom, 0.0)
        memb = jnp.where(gate_ref[...] > 0.0, 1.0, 0.0)
        r = lax.broadcasted_iota(jnp.int32, (tm, tm), 0)
        c = lax.broadcasted_iota(jnp.int32, (tm, tm), 1)
        rank_ref[...] = jnp.dot(jnp.where(c < r, 1.0, 0.0).astype(BF16), memb.astype(BF16),
                                preferred_element_type=F32)
        membt = memb.T
        membt_ref[...] = membt
        rankt_ref[...] = jnp.dot(membt.astype(BF16), jnp.where(r < c, 1.0, 0.0).astype(BF16),
                                 preferred_element_type=F32)

    def expert_column(ref):
        lane = lax.broadcasted_iota(jnp.int32, ref.shape, 1)
        return jnp.sum(jnp.where(lane == e, ref[...], 0.0), axis=-1, keepdims=True)

    @pl.when(j == 0)
    def _():
        cnt_ref[0] = jnp.sum(jnp.where(expert_column(gate_ref) > 0.0, 1.0, 0.0)).astype(jnp.int32)

    n_chunks = (cnt_ref[0] + (cap - 1)) // cap

    @pl.when(j == 0)
    def _():
        rt = rankt_ref[pl.ds(e, 1), :]
        mt = membt_ref[pl.ds(e, 1), :]

        def gather(ci, carry):
            base = pl.multiple_of(ci * cap, cap)
            slot = (lax.broadcasted_iota(jnp.int32, (cap, tm), 0) + base).astype(F32)
            onehot = jnp.where(jnp.logical_and(slot == rt, mt > 0.5), 1.0, 0.0).astype(BF16)
            xc_ref[pl.ds(base, cap), :] = jnp.dot(onehot, xn_ref[...], preferred_element_type=F32).astype(BF16)
            yc_ref[pl.ds(base, cap), :] = jnp.zeros((cap, yc_ref.shape[1]), F32)
            return carry

        lax.fori_loop(0, n_chunks, gather, 0)

    def ffn(ci, carry):
        rows = pl.ds(pl.multiple_of(ci * cap, cap), cap)
        hid = _swiglu_hidden(xc_ref[rows, :], wg_ref[0], wu_ref[0])
        yc_ref[rows, :] += jnp.dot(hid.astype(BF16), wd_ref[0], preferred_element_type=F32)
        return carry

    lax.fori_loop(0, n_chunks, ffn, 0)

    @pl.when(j == last_j)
    def _():
        rank_e = expert_column(rank_ref)
        gate_e = expert_column(gate_ref)

        def scatter(ci, carry):
            base = pl.multiple_of(ci * cap, cap)
            slot = (lax.broadcasted_iota(jnp.int32, (tm, cap), 1) + base).astype(F32)
            onehot_t = jnp.where(slot == rank_e, gate_e, 0.0).astype(BF16)
            o_ref[...] += jnp.dot(onehot_t, yc_ref[pl.ds(base, cap), :].astype(BF16), preferred_element_type=F32)
            return carry

        lax.fori_loop(0, n_chunks, scatter, 0)


def moe_ffn(x, g, router_padded, wg, wu, wd):
    n, d = x.shape
    n_exp, _, f = wg.shape
    tm = min(ROW_TILE, n)
    tf = MOE_FF_TILE if f % MOE_FF_TILE == 0 else FF_TILE
    cap = min(MOE_CHUNK, tm)
    cap_rows = -(-tm // cap) * cap
    assert n % tm == 0 and f % tf == 0 and n_exp == N_EXPERTS and tm % LANES == 0 and cap % (2 * SUBLANES) == 0
    return pl.pallas_call(
        functools.partial(_moe_ffn_kernel, cap=cap),
        grid=(n // tm, n_exp, f // tf),
        in_specs=[pl.BlockSpec((tm, d), lambda i, e, j: (i, 0)),
                  pl.BlockSpec((1, d), lambda i, e, j: (0, 0)),
                  pl.BlockSpec((d, LANES), lambda i, e, j: (0, 0)),
                  pl.BlockSpec((1, d, tf), lambda i, e, j: (e, 0, j)),
                  pl.BlockSpec((1, d, tf), lambda i, e, j: (e, 0, j)),
                  pl.BlockSpec((1, tf, d), lambda i, e, j: (e, j, 0))],
        out_specs=pl.BlockSpec((tm, d), lambda i, e, j: (i, 0)),
        out_shape=jax.ShapeDtypeStruct((n, d), F32),
        scratch_shapes=[pltpu.VMEM((tm, d), BF16),
                        pltpu.VMEM((tm, LANES), F32),
                        pltpu.VMEM((tm, LANES), F32),
                        pltpu.VMEM((LANES, tm), F32),
                        pltpu.VMEM((LANES, tm), F32),
                        pltpu.VMEM((cap_rows, d), BF16),
                        pltpu.VMEM((cap_rows, d), F32),
                        pltpu.SMEM((1,), jnp.int32)],
        compiler_params=_params(("arbitrary", "arbitrary", "arbitrary")),
        name="moe_ffn",
    )(x, g.reshape(1, d), router_padded, wg, wu, wd)


def kernel(x_prompt, x_sample, mem_prompt, cache_a_k, cache_a_v, cache_b_k, cache_b_v, cache_mem_k, cache_mem_v, page_table, rel_bias, norm_mix, w_in, lambda_q1, lambda_k1, lambda_q2, lambda_k2, subln_gain, w_br_a, w_br_b, w_out, norm_mem_q, norm_mem_kv, w_mem_q, w_mem_kv, w_mem_o, norm_ffn, ffn_w_gate, ffn_w_up, ffn_w_down, moe_router, moe_w_gate, moe_w_up, moe_w_down, norm_final):
    bsz, seq, d = x_prompt.shape
    n_seq, r_new, _ = x_sample.shape
    n_mem = mem_prompt.shape[1]
    depth = w_in.shape[0]

    cb_kt = jnp.transpose(cache_b_k, (0, 2, 3, 4, 1))
    cb_vt = jnp.transpose(cache_b_v, (0, 2, 3, 4, 1))
    cm_k = cache_mem_k.reshape(n_seq, n_mem, depth * W_M)
    cm_v = cache_mem_v.reshape(n_seq, n_mem, depth * W_M)

    hp = x_prompt.reshape(bsz * seq, d)
    hs = x_sample.reshape(n_seq * r_new, d)
    zs_p, zs_s, mkvs = [], [], []
    for l in range(depth):
        lam_init = 0.8 - 0.6 * math.exp(-0.3 * l)
        lamv = jnp.stack([lambda_q1[l], lambda_k1[l], lambda_q2[l], lambda_k2[l]]).astype(F32)
        w_in_l = w_in[l].astype(BF16)
        wa, wb, wo = w_br_a[l].astype(BF16), w_br_b[l].astype(BF16), w_out[l].astype(BF16)

        zp = norm_matmul(hp, norm_mix[l], w_in_l)
        zp3 = zp.reshape(bsz, seq, IN_COLS)
        oa = diff_attention_prompt(zp3, rel_bias, lamv, subln_gain[l], lam_init)
        ob = moba_attention_prompt(zp3, rel_bias)
        hp = mix_out(oa.reshape(bsz * seq, W_A), ob.reshape(bsz * seq, W_B), zp, hp, wa, wb, wo)
        zs_p.append(zp3)

        zs = norm_matmul(hs, norm_mix[l], w_in_l)
        zs3 = zs.reshape(n_seq, r_new, IN_COLS)
        oa_s = diff_attention_decode(zs3, cache_a_k, cache_a_v, page_table, l, rel_bias, lamv, subln_gain[l], lam_init)
        ob_s = moba_attention_decode(zs3, cb_kt, cb_vt, page_table, l, rel_bias)
        hs = mix_out(oa_s.reshape(n_seq * r_new, W_A), ob_s.reshape(n_seq * r_new, W_B), zs, hs, wa, wb, wo)
        zs_s.append(zs3)

        mkv = norm_matmul(mem_prompt.reshape(bsz * n_mem, d), norm_mem_kv[l], w_mem_kv[l].astype(BF16))
        mkv3 = mkv.reshape(bsz, n_mem, 2 * W_M)
        mkvs.append(mkv3)
        wq, wmo = w_mem_q[l].astype(BF16), w_mem_o[l].astype(BF16)
        hp = memory_attention(hp.reshape(bsz, seq, d), norm_mem_q[l], mkv3, 0, mkv3, 1, wq, wmo).reshape(bsz * seq, d)
        hs = memory_attention(hs.reshape(n_seq, r_new, d), norm_mem_q[l], cm_k, l, cm_v, l, wq, wmo).reshape(n_seq * r_new, d)

        i = l // 2
        if l % 2 == 0:
            wg, wu, wd = ffn_w_gate[i].astype(BF16), ffn_w_up[i].astype(BF16), ffn_w_down[i].astype(BF16)
            hp = dense_ffn(hp, norm_ffn[l], wg, wu, wd)
            hs = dense_ffn(hs, norm_ffn[l], wg, wu, wd)
        else:
            wg, wu, wd = moe_w_gate[i].astype(BF16), moe_w_up[i].astype(BF16), moe_w_down[i].astype(BF16)
            router = jnp.pad(moe_router[i].astype(F32), ((0, 0), (0, LANES - N_EXPERTS)))
            hp = moe_ffn(hp, norm_ffn[l], router, wg, wu, wd)
            hs = moe_ffn(hs, norm_ffn[l], router, wg, wu, wd)

    y_prompt = rmsnorm(hp, norm_final).reshape(bsz, seq, d)
    y_sample = rmsnorm(hs, norm_final).reshape(n_seq, r_new, d)

    def stacked(zs, col, width, heads):
        rows = jnp.stack([z[:, :, col:col + width] for z in zs], axis=2)
        return rows.reshape(rows.shape[0], rows.shape[1], depth, heads, width // heads)

    new_mem_k = jnp.stack([m[:, :, :W_M] for m in mkvs], axis=2).reshape(bsz, n_mem, depth, H_M, HD_M)
    new_mem_v = jnp.stack([m[:, :, W_M:] for m in mkvs], axis=2).reshape(bsz, n_mem, depth, H_M, HD_M)
    return (y_prompt, y_sample,
            stacked(zs_p, COL_KA, W_A, H_A), stacked(zs_p, COL_VA, W_A, H_A),
            stacked(zs_p, COL_KB, W_B, H_B), stacked(zs_p, COL_VB, W_B, H_B),
            new_mem_k, new_mem_v,
            stacked(zs_s, COL_KA, W_A, H_A), stacked(zs_s, COL_VA, W_A, H_A),
            stacked(zs_s, COL_KB, W_B, H_B), stacked(zs_s, COL_VB, W_B, H_B))
```
